```python
import jax, jax.numpy as jnp
from jax import lax
import numpy as np

D_MODEL = 1024
BATCH = 16
SEQ = 256
DEPTH = 4
DEC_BATCH = 8
DEC_SEQ = 2048
PAST_LEN = 256

GRID_W = 64
N_AB_LAYERS = (DEPTH + 1) // 2
N_F_LAYERS = DEPTH // 2
A_HEADS = 8
A_KV_HEADS = 2
A_GROUP = A_HEADS // A_KV_HEADS
A_HEAD_DIM = 64
WINDOW = 128
BAND = 128
B_HEADS = 8
B_NOPE_DIM = 64
B_ROPE_DIM = 32
B_V_DIM = 64
B_Q_LORA = 384
B_KV_LORA = 256
A_Q_W = A_HEADS * A_HEAD_DIM
A_KV_W = A_KV_HEADS * A_HEAD_DIM
AB_SPLITS = (A_Q_W, A_Q_W + A_KV_W, A_Q_W + 2 * A_KV_W, A_Q_W + 2 * A_KV_W + B_Q_LORA,
             A_Q_W + 2 * A_KV_W + B_Q_LORA + B_KV_LORA)
AB_IN = AB_SPLITS[-1] + B_ROPE_DIM
AB_MIX = A_HEADS * A_HEAD_DIM + B_HEADS * B_V_DIM
A_SCALE = A_HEAD_DIM ** -0.5
B_SCALE = (B_NOPE_DIM + B_ROPE_DIM) ** -0.5
F_GROUPS = 4
F_WIDTH = D_MODEL
F_GROUP_DIM = F_WIDTH // F_GROUPS
D_FF = 2816
FFN_HALF = 0.5
ALPHA = (2 * DEPTH) ** 0.25
BETA = (8 * DEPTH) ** -0.25
LN_EPS = 1e-5
RMS_EPS = 1e-6
N_MODS = 9
Q_BLOCK = 128
ROPE_BASE = 10000.0
NEG_INF = -1e30

kernel_name = "hybrid_dit_swa_mla_fnet_macaron_step"


def layer_norm(x, g, b):
    xf = x.astype(jnp.float32)
    xc = xf - jnp.mean(xf, axis=-1, keepdims=True)
    var = jnp.mean(xc * xc, axis=-1, keepdims=True)
    return (xc * lax.rsqrt(var + LN_EPS) * g.astype(jnp.float32) + b.astype(jnp.float32)).astype(x.dtype)


def rms_norm(x, g):
    xf = x.astype(jnp.float32)
    y = xf * lax.rsqrt(jnp.mean(xf * xf, axis=-1, keepdims=True) + RMS_EPS)
    return (y * g.astype(jnp.float32)).astype(x.dtype)


def ada_mods(cond, w, b):
    return jnp.split(jax.nn.silu(cond) @ w + b, N_MODS, axis=-1)


def modulate(x, shift, scale):
    return x * (1.0 + scale[:, None, :]) + shift[:, None, :]


def post_norm(x, out, gate, g, b):
    return layer_norm(ALPHA * x + gate[:, None, :] * out, g, b)


def swiglu(h, w_gate, w_up, w_down):
    return (jax.nn.silu(h @ w_gate) * (h @ w_up)) @ w_down


def half_ffn(x, shift, scale, gate, w_gate, w_up, w_down, g, b):
    h = modulate(x, shift, scale)
    return post_norm(x, FFN_HALF * swiglu(h, w_gate, w_up, w_down), gate, g, b)


def rope_1d(x, pos):
    half = x.shape[-1] // 2
    inv = ROPE_BASE ** (-jnp.arange(half, dtype=jnp.float32) / half)
    ang = pos.astype(jnp.float32)[:, None] * inv[None, :]
    ang = ang.reshape((ang.shape[0],) + (1,) * (x.ndim - 3) + (half,))
    cos = jnp.cos(ang).astype(x.dtype)
    sin = jnp.sin(ang).astype(x.dtype)
    x1, x2 = x[..., :half], x[..., half:]
    return jnp.concatenate([x1 * cos - x2 * sin, x2 * cos + x1 * sin], axis=-1)


def axial_rope(x):
    L = x.shape[1]
    rows = L // GRID_W
    row = jnp.repeat(jnp.arange(rows), GRID_W)
    col = jnp.tile(jnp.arange(GRID_W), rows)
    h = x.shape[-1] // 2
    return jnp.concatenate([rope_1d(x[..., :h], row), rope_1d(x[..., h:], col)], axis=-1)


def sink_softmax(s, sink):
    m = jnp.maximum(jnp.max(s, axis=-1, keepdims=True), sink)
    p = jnp.exp(s - m)
    return p / (jnp.sum(p, axis=-1, keepdims=True) + jnp.exp(sink - m))


def dense_attention(q, k, v, scale, sink):
    bsz, lq, hk, g, dk = q.shape
    qb = q.reshape(bsz, lq // Q_BLOCK, Q_BLOCK, hk, g, dk).swapaxes(0, 1)

    def block(qi):
        s = jnp.einsum("bqhgd,bkhd->bhgqk", qi, k).astype(jnp.float32) * scale
        if sink is None:
            p = jax.nn.softmax(s, axis=-1)
        else:
            p = sink_softmax(s, sink.astype(jnp.float32)[None, :, :, None, None])
        return jnp.einsum("bhgqk,bkhd->bqhgd", p.astype(v.dtype), v)

    o = lax.map(block, qb)
    return o.swapaxes(0, 1).reshape(bsz, lq, hk, g, v.shape[-1])


def banded_attention(q, k, v, kc, vc, scale, sink):
    bsz, L, hk, g, dk = q.shape
    nb = L // BAND
    pad = ((0, 0), (BAND, BAND), (0, 0), (0, 0))
    kp = jnp.pad(k, pad).reshape(bsz, nb + 2, BAND, hk, dk)
    vp = jnp.pad(v, pad).reshape(bsz, nb + 2, BAND, hk, v.shape[-1])
    kw = jnp.concatenate([kp[:, :-2], kp[:, 1:-1], kp[:, 2:]], axis=2)
    vw = jnp.concatenate([vp[:, :-2], vp[:, 1:-1], vp[:, 2:]], axis=2)
    qb = q.reshape(bsz, nb, BAND, hk, g, dk)
    s_loc = jnp.einsum("bnqhgd,bnkhd->bhgnqk", qb, kw).astype(jnp.float32) * scale
    qpos = jnp.arange(nb)[:, None] * BAND + jnp.arange(BAND)[None, :]
    kpos = (jnp.arange(nb)[:, None] * BAND - BAND + jnp.arange(3 * BAND)[None, :])[:, None, :]
    valid = (kpos >= 0) & (kpos < L) & (jnp.abs(kpos - qpos[:, :, None]) <= WINDOW)
    s_loc = jnp.where(valid, s_loc, NEG_INF)
    s_ctx = jnp.einsum("bnqhgd,bchd->bhgnqc", qb, kc).astype(jnp.float32) * scale
    lc = kc.shape[1]
    p = sink_softmax(jnp.concatenate([s_ctx, s_loc], axis=-1),
                     sink.astype(jnp.float32)[None, :, :, None, None, None]).astype(v.dtype)
    o = (jnp.einsum("bhgnqc,bchd->bnqhgd", p[..., :lc], vc)
         + jnp.einsum("bhgnqk,bnkhd->bnqhgd", p[..., lc:], vw))
    return o.reshape(bsz, L, hk, g, v.shape[-1])


def split_ab(proj):
    bsz, L, _ = proj.shape
    qa, ka, va, cq, ckv, kpe = jnp.split(proj, AB_SPLITS, axis=-1)
    qa = qa.reshape(bsz, L, A_KV_HEADS, A_GROUP, A_HEAD_DIM)
    ka = ka.reshape(bsz, L, A_KV_HEADS, A_HEAD_DIM)
    va = va.reshape(bsz, L, A_KV_HEADS, A_HEAD_DIM)
    return qa, ka, va, cq, ckv, kpe


def mla_q(cq, g_cq, w_uq):
    bsz, L, _ = cq.shape
    return (rms_norm(cq, g_cq) @ w_uq).reshape(bsz, L, B_HEADS, B_NOPE_DIM + B_ROPE_DIM)


def mla_kv(ckv_n, w_ukv):
    bsz, L, _ = ckv_n.shape
    kv = (ckv_n @ w_ukv).reshape(bsz, L, B_HEADS, B_NOPE_DIM + B_V_DIM)
    return kv[..., :B_NOPE_DIM], kv[..., B_NOPE_DIM:]


def mla_keys(k_nope, kpe):
    kpe_h = jnp.broadcast_to(kpe[:, :, None, :], k_nope.shape[:3] + (B_ROPE_DIM,))
    return jnp.concatenate([k_nope, kpe_h], axis=-1)


def merge_heads(o_a, o_b, w_out):
    bsz, L = o_a.shape[:2]
    return jnp.concatenate([o_a.reshape(bsz, L, -1), o_b.reshape(bsz, L, -1)], axis=-1) @ w_out


def ab_mixer_context(h, w_in, w_out, sink, g_cq, w_uq, g_ckv, w_ukv):
    qa, ka, va, cq, ckv, kpe = split_ab(h @ w_in)
    ckv_n = rms_norm(ckv, g_ckv)
    qb = mla_q(cq, g_cq, w_uq)
    k_nope, vb = mla_kv(ckv_n, w_ukv)
    o_a = dense_attention(qa, ka, va, A_SCALE, sink.reshape(A_KV_HEADS, A_GROUP))
    o_b = dense_attention(qb[:, :, :, None, :], mla_keys(k_nope, kpe), vb, B_SCALE, None)
    return merge_heads(o_a, o_b, w_out), ka, va, ckv_n, kpe


def ab_mixer_latent(h, ck_a, cv_a, c_ckv, c_kpe, w_in, w_out, sink, g_cq, w_uq, g_ckv, w_ukv):
    qa, ka, va, cq, ckv, kpe = split_ab(h @ w_in)
    o_a = banded_attention(axial_rope(qa), axial_rope(ka), va, ck_a, cv_a, A_SCALE,
                           sink.reshape(A_KV_HEADS, A_GROUP))
    qb = mla_q(cq, g_cq, w_uq)
    qb = jnp.concatenate([qb[..., :B_NOPE_DIM], axial_rope(qb[..., B_NOPE_DIM:])], axis=-1)
    k_nope, vb = mla_kv(rms_norm(ckv, g_ckv), w_ukv)
    kc_nope, vc = mla_kv(c_ckv, w_ukv)
    k_all = jnp.concatenate([mla_keys(kc_nope, c_kpe), mla_keys(k_nope, axial_rope(kpe))], axis=1)
    v_all = jnp.concatenate([vc, vb], axis=1)
    o_b = dense_attention(qb[:, :, :, None, :], k_all, v_all, B_SCALE, None)
    return merge_heads(o_a, o_b, w_out)


def fourier_mixer(h, w_in, w_out):
    bsz, L, _ = h.shape
    u = (h @ w_in).reshape(bsz, L, F_GROUPS, F_GROUP_DIM).astype(jnp.float32)
    f = jnp.fft.fft2(u, axes=(1, 3), norm="ortho").real
    return f.astype(h.dtype).reshape(bsz, L, F_WIDTH) @ w_out


def setup_inputs(seed: int = 0) -> dict:
    key = jax.random.key(seed)
    ks = jax.random.split(key, 26)

    def nrm(k, shape, s):
        return s * jax.random.normal(k, shape, jnp.float32)

    return {
        "x_prompt": nrm(ks[0], (BATCH, SEQ, D_MODEL), 1.0),
        "x_sample": nrm(ks[1], (DEC_BATCH, DEC_SEQ, D_MODEL), 1.0),
        "cache_a_k": nrm(ks[2], (DEC_BATCH, N_AB_LAYERS, PAST_LEN, A_KV_HEADS, A_HEAD_DIM), 1.0),
        "cache_a_v": nrm(ks[3], (DEC_BATCH, N_AB_LAYERS, PAST_LEN, A_KV_HEADS, A_HEAD_DIM), 1.0),
        "cache_b_ckv": nrm(ks[4], (DEC_BATCH, N_AB_LAYERS, PAST_LEN, B_KV_LORA), 1.0),
        "cache_b_kpe": nrm(ks[5], (DEC_BATCH, N_AB_LAYERS, PAST_LEN, B_ROPE_DIM), 1.0),
        "c": nrm(ks[6], (DEC_BATCH, D_MODEL), 1.0),
        "c_ctx": nrm(ks[7], (D_MODEL,), 1.0),
        "w_ada": nrm(ks[8], (DEPTH, D_MODEL, N_MODS * D_MODEL), 0.5 * D_MODEL ** -0.5),
        "b_ada": nrm(ks[9], (DEPTH, N_MODS * D_MODEL), 0.02),
        "ln_g": 1.0 + nrm(ks[10], (DEPTH, 3, D_MODEL), 0.02),
        "ln_b": nrm(ks[11], (DEPTH, 3, D_MODEL), 0.02),
        "ffn_w_gate": nrm(ks[12], (DEPTH, 2, D_MODEL, D_FF), D_MODEL ** -0.5),
        "ffn_w_up": nrm(ks[13], (DEPTH, 2, D_MODEL, D_FF), D_MODEL ** -0.5),
        "ffn_w_down": nrm(ks[14], (DEPTH, 2, D_FF, D_MODEL), BETA * D_FF ** -0.5),
        "ab_w_in": nrm(ks[15], (N_AB_LAYERS, D_MODEL, AB_IN), D_MODEL ** -0.5),
        "ab_w_out": nrm(ks[16], (N_AB_LAYERS, AB_MIX, D_MODEL), BETA * AB_MIX ** -0.5),
        "a_sink": nrm(ks[17], (N_AB_LAYERS, A_HEADS), 0.5),
        "b_g_cq": 1.0 + nrm(ks[18], (N_AB_LAYERS, B_Q_LORA), 0.02),
        "b_w_uq": nrm(ks[19], (N_AB_LAYERS, B_Q_LORA, B_HEADS * (B_NOPE_DIM + B_ROPE_DIM)), B_Q_LORA ** -0.5),
        "b_g_ckv": 1.0 + nrm(ks[20], (N_AB_LAYERS, B_KV_LORA), 0.02),
        "b_w_ukv": nrm(ks[21], (N_AB_LAYERS, B_KV_LORA, B_HEADS * (B_NOPE_DIM + B_V_DIM)), B_KV_LORA ** -0.5),
        "f_w_in": nrm(ks[22], (N_F_LAYERS, D_MODEL, F_WIDTH), D_MODEL ** -0.5),
        "f_w_out": nrm(ks[23], (N_F_LAYERS, F_WIDTH, D_MODEL), BETA * F_WIDTH ** -0.5),
    }


def reference(x_prompt, x_sample, cache_a_k, cache_a_v, cache_b_ckv, cache_b_kpe, c, c_ctx,
              w_ada, b_ada, ln_g, ln_b, ffn_w_gate, ffn_w_up, ffn_w_down, ab_w_in, ab_w_out,
              a_sink, b_g_cq, b_w_uq, b_g_ckv, b_w_ukv, f_w_in, f_w_out):
    y_p = x_prompt
    y_s = x_sample
    new_k, new_v, new_ckv, new_kpe = [], [], [], []
    for l in range(DEPTH):
        mc = ada_mods(c_ctx[None, :], w_ada[l], b_ada[l])
        ms = ada_mods(c, w_ada[l], b_ada[l])
        ffn_pre = (ffn_w_gate[l, 0], ffn_w_up[l, 0], ffn_w_down[l, 0], ln_g[l, 0], ln_b[l, 0])
        y_p = half_ffn(y_p, mc[0], mc[1], mc[2], *ffn_pre)
        y_s = half_ffn(y_s, ms[0], ms[1], ms[2], *ffn_pre)
        h_p = modulate(y_p, mc[3], mc[4])
        h_s = modulate(y_s, ms[3], ms[4])
        i = l // 2
        if l % 2 == 0:
            mla_w = (b_g_cq[i], b_w_uq[i], b_g_ckv[i], b_w_ukv[i])
            o_p, ka, va, ckv_n, kpe = ab_mixer_context(h_p, ab_w_in[i], ab_w_out[i], a_sink[i], *mla_w)
            o_s = ab_mixer_latent(h_s, cache_a_k[:, i], cache_a_v[:, i], cache_b_ckv[:, i], cache_b_kpe[:, i],
                                  ab_w_in[i], ab_w_out[i], a_sink[i], *mla_w)
            new_k.append(ka)
            new_v.append(va)
            new_ckv.append(ckv_n)
            new_kpe.append(kpe)
        else:
            o_p = fourier_mixer(h_p, f_w_in[i], f_w_out[i])
            o_s = fourier_mixer(h_s, f_w_in[i], f_w_out[i])
        y_p = post_norm(y_p, o_p, mc[5], ln_g[l, 1], ln_b[l, 1])
        y_s = post_norm(y_s, o_s, ms[5], ln_g[l, 1], ln_b[l, 1])
        ffn_post = (ffn_w_gate[l, 1], ffn_w_up[l, 1], ffn_w_down[l, 1], ln_g[l, 2], ln_b[l, 2])
        y_p = half_ffn(y_p, mc[6], mc[7], mc[8], *ffn_post)
        y_s = half_ffn(y_s, ms[6], ms[7], ms[8], *ffn_post)
    y_prompt = y_p
    y_sample = y_s
    new_a_k = jnp.stack(new_k, axis=1)
    new_a_v = jnp.stack(new_v, axis=1)
    new_b_ckv = jnp.stack(new_ckv, axis=1)
    new_b_kpe = jnp.stack(new_kpe, axis=1)
    return (y_prompt, y_sample, new_a_k, new_a_v, new_b_ckv, new_b_kpe)
```

```python
import functools

import numpy as np
import jax
import jax.numpy as jnp
from jax import lax
from jax.experimental import pallas as pl
from jax.experimental.pallas import tpu as pltpu

F32 = jnp.float32
BF16 = jnp.bfloat16

D = 1024
BATCH = 16
SEQ = 256
DEPTH = 4
DEC_BATCH = 8
DEC_SEQ = 2048
PAST = 256
GRID_W = 64
N_AB = (DEPTH + 1) // 2
A_HEADS = 8
A_KV_HEADS = 2
A_HEAD_DIM = 64
WINDOW = 128
BAND = 128
B_HEADS = 8
B_NOPE = 64
B_ROPE = 32
B_V = 64
B_Q_LORA = 384
B_KV_LORA = 256
AB_IN = 1440
AB_IN_PAD = 1536
A_SCALE = A_HEAD_DIM ** -0.5
B_SCALE = (B_NOPE + B_ROPE) ** -0.5
F_GROUPS = 4
F_GROUP_DIM = D // F_GROUPS
D_FF = 2816
FFN_HALF = 0.5
ALPHA = (2 * DEPTH) ** 0.25
LN_EPS = 1e-5
RMS_EPS = 1e-6
N_MODS = 9
ROPE_BASE = 10000.0
NEG_INF = -1e30

NS = DEC_BATCH * DEC_SEQ
NP = BATCH * SEQ
NTOK = NS + NP
MOD_ROWS = 16
CTX_ROW = DEC_BATCH
LANES = 128
VMEM_LIMIT = 56 * 1024 * 1024


def _cparams(*sem):
    return pltpu.CompilerParams(dimension_semantics=sem, vmem_limit_bytes=VMEM_LIMIT)


def _silu(x):
    return x * (1.0 / (1.0 + jnp.exp(-x)))


def _layer_norm(z, g, b):
    mu = jnp.mean(z, axis=-1, keepdims=True)
    zc = z - mu
    var = jnp.mean(zc * zc, axis=-1, keepdims=True)
    return zc * lax.rsqrt(var + LN_EPS) * g + b


def _rms_norm(x, g):
    return x * lax.rsqrt(jnp.mean(x * x, axis=-1, keepdims=True) + RMS_EPS) * g


def _dot(a, b):
    return jnp.dot(a, b, preferred_element_type=F32)


def _dot_nt(a, b):
    return lax.dot_general(a, b, (((1,), (1,)), ((), ())), preferred_element_type=F32)


def _rope_tables(tm):
    pos = np.arange(DEC_SEQ)
    row = (pos // GRID_W).astype(np.float64)
    col = (pos % GRID_W).astype(np.float64)

    def layout(lane_d, quarter):
        cos = np.ones((DEC_SEQ, LANES))
        sin_m = np.zeros((DEC_SEQ, LANES))
        sin_p = np.zeros((DEC_SEQ, LANES))
        for lane in range(LANES):
            d = lane_d[lane]
            if d < 0:
                continue
            q, j = d // quarter, d % quarter
            inv = ROPE_BASE ** (-float(j) / quarter)
            ang = (row if q < 2 else col) * inv
            cos[:, lane] = np.cos(ang)
            if q % 2 == 0:
                sin_m[:, lane] = -np.sin(ang)
            else:
                sin_p[:, lane] = np.sin(ang)
        return [cos, sin_m, sin_p]

    lane_a = [lane % A_HEAD_DIM for lane in range(LANES)]
    lane_b = [lane - B_NOPE if B_NOPE <= lane < B_NOPE + B_ROPE else -1 for lane in range(LANES)]
    lane_k = [lane if lane < B_ROPE else -1 for lane in range(LANES)]
    tabs = layout(lane_a, A_HEAD_DIM // 4) + layout(lane_b, B_ROPE // 4) + layout(lane_k, B_ROPE // 4)
    out = np.zeros((9, DEC_SEQ + tm, LANES), np.float32)
    for t, tab in enumerate(tabs):
        out[t, :DEC_SEQ] = tab
        out[t, DEC_SEQ:] = 1.0 if t % 3 == 0 else 0.0
    return out


def _dft_mats(n):
    k = np.arange(n)
    kn = (k[:, None] * k[None, :]) % n
    ang = 2.0 * np.pi * kn.astype(np.float64) / n
    return np.cos(ang), -np.sin(ang)


def _mods_kernel(c_ref, w_ref, b_ref, o_ref):
    x = _silu(c_ref[...]).astype(BF16)
    o_ref[...] = _dot(x, w_ref[...].astype(BF16)) + b_ref[...]


def _mods_call(cond, w_ada, b_ada):
    tn = 1024
    nw = N_MODS * D
    return pl.pallas_call(
        _mods_kernel,
        grid=(DEPTH, nw // tn),
        in_specs=[
            pl.BlockSpec((MOD_ROWS, D), lambda l, j: (0, 0)),
            pl.BlockSpec((None, D, tn), lambda l, j: (l, 0, j)),
            pl.BlockSpec((None, 1, tn), lambda l, j: (l, 0, j)),
        ],
        out_specs=pl.BlockSpec((None, MOD_ROWS, tn), lambda l, j: (l, 0, j)),
        out_shape=jax.ShapeDtypeStruct((DEPTH, MOD_ROWS, nw), F32),
        compiler_params=_cparams("arbitrary", "arbitrary"),
        name="ada_mods",
    )(cond, w_ada, b_ada.reshape(DEPTH, 1, nw))


def _mod_spec(l, m, tm, ngrid):
    def idx(*g):
        r = jnp.minimum((g[0] * tm) // DEC_SEQ, CTX_ROW)
        return (l * MOD_ROWS * N_MODS + r * N_MODS + m, 0, 0)
    del ngrid
    return pl.BlockSpec((None, 1, D), idx)


def _ln_spec(l, which):
    return pl.BlockSpec((None, 1, D), lambda *g: (l * 3 + which, 0, 0))


def _ffn_kernel(x_ref, sh_ref, sc_ref, gt_ref, wg_ref, wu_ref, wd_ref, g_ref, b_ref, o_ref, h_scr, acc_scr):
    j = pl.program_id(1)

    @pl.when(j == 0)
    def _():
        h_scr[...] = (x_ref[...] * (1.0 + sc_ref[...]) + sh_ref[...]).astype(BF16)
        acc_scr[...] = jnp.zeros_like(acc_scr)

    h = h_scr[...]
    a = _silu(_dot(h, wg_ref[...])) * _dot(h, wu_ref[...])
    acc_scr[...] += _dot(a.astype(BF16), wd_ref[...])

    @pl.when(j == pl.num_programs(1) - 1)
    def _():
        z = ALPHA * x_ref[...] + gt_ref[...] * (FFN_HALF * acc_scr[...])
        o_ref[...] = _layer_norm(z, g_ref[...], b_ref[...])


def _ffn_call(y, modsr, lng, lnb, wg, wu, wd, l, k):
    tm, tf = 512, 256
    m0 = 0 if k == 0 else 6
    which = 0 if k == 0 else 2
    return pl.pallas_call(
        _ffn_kernel,
        grid=(NTOK // tm, D_FF // tf),
        in_specs=[
            pl.BlockSpec((tm, D), lambda i, j: (i, 0)),
            _mod_spec(l, m0, tm, 2), _mod_spec(l, m0 + 1, tm, 2), _mod_spec(l, m0 + 2, tm, 2),
            pl.BlockSpec((None, None, D, tf), lambda i, j: (l, k, 0, j)),
            pl.BlockSpec((None, None, D, tf), lambda i, j: (l, k, 0, j)),
            pl.BlockSpec((None, None, tf, D), lambda i, j: (l, k, j, 0)),
            _ln_spec(l, which), _ln_spec(l, which),
        ],
        out_specs=pl.BlockSpec((tm, D), lambda i, j: (i, 0)),
        out_shape=jax.ShapeDtypeStruct((NTOK, D), F32),
        scratch_shapes=[pltpu.VMEM((tm, D), BF16), pltpu.VMEM((tm, D), F32)],
        compiler_params=_cparams("arbitrary", "arbitrary"),
        name="half_ffn",
    )(y, modsr, modsr, modsr, wg, wu, wd, lng, lnb)


def _rope(x, cos, sin_m, sin_p, quarter):
    n = x.shape[1] // LANES
    up = pltpu.roll(x, x.shape[1] - quarter, 1)
    dn = pltpu.roll(x, quarter, 1)
    if n > 1:
        cos = jnp.concatenate([cos] * n, axis=1)
        sin_m = jnp.concatenate([sin_m] * n, axis=1)
        sin_p = jnp.concatenate([sin_p] * n, axis=1)
    return x * cos + up * sin_m + dn * sin_p


def _inproj_kernel(ns_tiles, x_ref, sh_ref, sc_ref, w_in_ref, gq_ref, wq_ref, gkv_ref, wk_ref, wv_ref, rope_ref,
                   qa_ref, ka_ref, va_ref, qb_ref, kb_ref, vb_ref,
                   cka_ref, cva_ref, cckv_ref, ckpe_ref):
    i = pl.program_id(0)
    h = (x_ref[...] * (1.0 + sc_ref[...]) + sh_ref[...]).astype(BF16)
    proj = _dot(h, w_in_ref[...])
    qa = proj[:, 0:512]
    ka = proj[:, 512:640]
    va = proj[:, 640:768]
    cq = proj[:, 768:1152]
    ckv = proj[:, 1152:1408]
    kpe = proj[:, 1408:1536]
    ckv_n = _rms_norm(ckv, gkv_ref[...])

    @pl.when(i >= ns_tiles)
    def _():
        cka_ref[...] = ka
        cva_ref[...] = va
        cckv_ref[...] = ckv_n
        ckpe_ref[...] = kpe[:, :B_ROPE]

    lane = lax.broadcasted_iota(jnp.int32, (1, LANES), 1)
    lo = lane < A_HEAD_DIM

    qa_ref[...] = (_rope(qa, rope_ref[0], rope_ref[1], rope_ref[2], A_HEAD_DIM // 4) * A_SCALE).astype(BF16)
    ka_r = _rope(ka, rope_ref[0], rope_ref[1], rope_ref[2], A_HEAD_DIM // 4)
    ka_s = pltpu.roll(ka_r, A_HEAD_DIM, 1)
    zero = jnp.zeros_like(ka_r)
    ka_ref[...] = jnp.concatenate(
        [jnp.where(lo, ka_r, zero), jnp.where(lo, zero, ka_s), jnp.where(lo, ka_s, zero), jnp.where(lo, zero, ka_r)],
        axis=1).astype(BF16)
    va_s = pltpu.roll(va, A_HEAD_DIM, 1)
    va_ref[...] = jnp.concatenate([jnp.where(lo, va, va_s), jnp.where(lo, va_s, va)], axis=1).astype(BF16)

    cq_n = _rms_norm(cq, gq_ref[...]).astype(BF16)
    qb = _dot(cq_n, wq_ref[...])
    qb_ref[...] = _rope(qb, rope_ref[3], rope_ref[4], rope_ref[5], B_ROPE // 4).astype(BF16)
    kpe_r = _rope(kpe, rope_ref[6], rope_ref[7], rope_ref[8], B_ROPE // 4)
    lat = jnp.concatenate([ckv_n.astype(BF16), kpe_r.astype(BF16)], axis=1)
    kb_ref[...] = _dot(lat, wk_ref[...]).astype(BF16)
    vb_ref[...] = _dot(lat[:, :B_KV_LORA], wv_ref[...]).astype(BF16)


def _inproj_call(y, modsr, l, w_in, g_cq, w_uq, g_ckv, w_k, w_v, rope_tabs, tm):
    ns_tiles = NS // tm
    pos_tiles = DEC_SEQ // tm

    def tok(w):
        return pl.BlockSpec((tm, w), lambda i: (i, 0))

    def ctx(w):
        return pl.BlockSpec((tm, w), lambda i: (jnp.maximum(i - ns_tiles, 0), 0))

    def full(a):
        return pl.BlockSpec(a.shape, lambda i: (0,) * a.ndim)

    def rope_idx(i):
        return (0, jnp.where(i < ns_tiles, i % pos_tiles, pos_tiles), 0)

    outs = pl.pallas_call(
        functools.partial(_inproj_kernel, ns_tiles),
        grid=(NTOK // tm,),
        in_specs=[
            tok(D), _mod_spec(l, 3, tm, 1), _mod_spec(l, 4, tm, 1),
            full(w_in), full(g_cq), full(w_uq), full(g_ckv), full(w_k), full(w_v),
            pl.BlockSpec((9, tm, LANES), rope_idx),
        ],
        out_specs=[tok(512), tok(512), tok(256), tok(1024), tok(1024), tok(512),
                   ctx(128), ctx(128), ctx(B_KV_LORA), ctx(B_ROPE)],
        out_shape=[
            jax.ShapeDtypeStruct((NTOK, 512), BF16), jax.ShapeDtypeStruct((NTOK, 512), BF16),
            jax.ShapeDtypeStruct((NTOK, 256), BF16), jax.ShapeDtypeStruct((NTOK, 1024), BF16),
            jax.ShapeDtypeStruct((NTOK, 1024), BF16), jax.ShapeDtypeStruct((NTOK, 512), BF16),
            jax.ShapeDtypeStruct((NP, 128), F32), jax.ShapeDtypeStruct((NP, 128), F32),
            jax.ShapeDtypeStruct((NP, B_KV_LORA), F32), jax.ShapeDtypeStruct((NP, B_ROPE), F32),
        ],
        compiler_params=_cparams("arbitrary"),
        name="ab_inproj",
    )(y, modsr, modsr, w_in, g_cq, w_uq, g_ckv, w_k, w_v, rope_tabs)
    return outs


def _ctxkv_kernel(ckv_ref, kpe_ref, wk_ref, wv_ref, k_ref, v_ref):
    ckv = ckv_ref[...].astype(BF16)
    k = _dot(ckv, wk_ref[:B_KV_LORA, :]) + _dot(kpe_ref[...].astype(BF16), wk_ref[B_KV_LORA:B_KV_LORA + B_ROPE, :])
    k_ref[...] = k.astype(BF16)
    v_ref[...] = _dot(ckv, wv_ref[...]).astype(BF16)


def _ctxkv_call(cache_ckv, cache_kpe, li, w_k, w_v):
    return pl.pallas_call(
        _ctxkv_kernel,
        grid=(DEC_BATCH,),
        in_specs=[
            pl.BlockSpec((None, None, PAST, B_KV_LORA), lambda b: (b, li, 0, 0)),
            pl.BlockSpec((None, None, PAST, B_ROPE), lambda b: (b, li, 0, 0)),
            pl.BlockSpec(w_k.shape, lambda b: (0, 0)),
            pl.BlockSpec(w_v.shape, lambda b: (0, 0)),
        ],
        out_specs=[pl.BlockSpec((PAST, 1024), lambda b: (b, 0)), pl.BlockSpec((PAST, 512), lambda b: (b, 0))],
        out_shape=[jax.ShapeDtypeStruct((DEC_BATCH * PAST, 1024), BF16),
                   jax.ShapeDtypeStruct((DEC_BATCH * PAST, 512), BF16)],
        compiler_params=_cparams("arbitrary"),
        name="b_ctx_kv",
    )(cache_ckv, cache_kpe, w_k, w_v)


def _softmax_pv(scores, values, sink):
    m = functools.reduce(jnp.maximum, [jnp.max(s, axis=-1, keepdims=True) for s in scores])
    if sink is not None:
        m = jnp.maximum(m, sink)
    den = None
    out = None
    for s, v in zip(scores, values):
        p = jnp.exp(s - m)
        d = jnp.sum(p, axis=-1, keepdims=True)
        o = _dot(p.astype(BF16), v)
        den = d if den is None else den + d
        out = o if out is None else out + o
    if sink is not None:
        den = den + jnp.exp(sink - m)
    return out * (1.0 / den)


def _lane_lo():
    return lax.broadcasted_iota(jnp.int32, (1, LANES), 1) < A_HEAD_DIM


def _attn_a_lat_kernel(sink_ref, q_ref, k_ref, v_ref, kc_ref, vc_ref, o_ref):
    n = pl.program_id(1)
    nb = pl.num_programs(1)
    lo = _lane_lo()
    kc = kc_ref[...]
    vc = vc_ref[...]
    kc_s = pltpu.roll(kc, A_HEAD_DIM, 1)
    vc_s = pltpu.roll(vc, A_HEAD_DIM, 1)
    zero = jnp.zeros_like(kc)
    kc_var = [jnp.where(lo, kc, zero), jnp.where(lo, zero, kc_s), jnp.where(lo, kc_s, zero), jnp.where(lo, zero, kc)]
    vc_dup = [jnp.where(lo, vc, vc_s), jnp.where(lo, vc_s, vc)]

    prev = pl.multiple_of(jnp.maximum(n - 1, 0) * BAND, BAND)
    cur = pl.multiple_of(n * BAND, BAND)
    nxt = pl.multiple_of(jnp.minimum(n + 1, nb - 1) * BAND, BAND)

    qi = lax.broadcasted_iota(jnp.int32, (2 * BAND, BAND), 0) % BAND
    kj = lax.broadcasted_iota(jnp.int32, (2 * BAND, BAND), 1)
    ok_prev = kj >= qi + jnp.where(n > 0, 0, BAND)
    ok_next = kj <= qi - jnp.where(n < nb - 1, 0, BAND)
    top = lax.broadcasted_iota(jnp.int32, (2 * BAND, 1), 0) < BAND

    for hk in range(A_KV_HEADS):
        q2 = jnp.concatenate([q_ref[:, (2 * hk) * LANES:(2 * hk + 1) * LANES],
                              q_ref[:, (2 * hk + 1) * LANES:(2 * hk + 2) * LANES]], axis=0)
        vs = [vc_dup[hk].astype(BF16)] + [v_ref[pl.ds(r, BAND), hk * LANES:(hk + 1) * LANES] for r in (prev, cur, nxt)]
        res = []
        for e in range(2):
            c0 = (2 * hk + e) * LANES
            s_ctx = _dot_nt(q2, kc_var[2 * hk + e].astype(BF16))
            s_prev = jnp.where(ok_prev, _dot_nt(q2, k_ref[pl.ds(prev, BAND), c0:c0 + LANES]), NEG_INF)
            s_cur = _dot_nt(q2, k_ref[pl.ds(cur, BAND), c0:c0 + LANES])
            s_next = jnp.where(ok_next, _dot_nt(q2, k_ref[pl.ds(nxt, BAND), c0:c0 + LANES]), NEG_INF)
            sink = jnp.where(top, sink_ref[4 * hk + e], sink_ref[4 * hk + 2 + e])
            res.append(_softmax_pv([s_ctx, s_prev, s_cur, s_next], vs, sink))
        o2 = jnp.where(lo, res[0], res[1]).astype(BF16)
        o_ref[:, (2 * hk) * LANES:(2 * hk + 1) * LANES] = o2[:BAND]
        o_ref[:, (2 * hk + 1) * LANES:(2 * hk + 2) * LANES] = o2[BAND:]


def _attn_a_lat_call(sink, qa, ka, va, cache_k, cache_v, li):
    nb = DEC_SEQ // BAND
    return pl.pallas_call(
        _attn_a_lat_kernel,
        grid=(DEC_BATCH, nb),
        in_specs=[
            pl.BlockSpec(memory_space=pltpu.SMEM),
            pl.BlockSpec((BAND, 512), lambda b, n: (b * nb + n, 0)),
            pl.BlockSpec((DEC_SEQ, 512), lambda b, n: (b, 0)),
            pl.BlockSpec((DEC_SEQ, 256), lambda b, n: (b, 0)),
            pl.BlockSpec((None, None, PAST, 128), lambda b, n: (b, li, 0, 0)),
            pl.BlockSpec((None, None, PAST, 128), lambda b, n: (b, li, 0, 0)),
        ],
        out_specs=pl.BlockSpec((BAND, 512), lambda b, n: (b * nb + n, 0)),
        out_shape=jax.ShapeDtypeStruct((NS, 512), BF16),
        compiler_params=_cparams("arbitrary", "arbitrary"),
        name="attn_a_latent",
    )(sink, qa, ka, va, cache_k, cache_v)


def _attn_b_lat_kernel(q_ref, k_ref, v_ref, kc_ref, vc_ref, o_ref):
    lo = _lane_lo()
    for j in range(B_HEADS // 2):
        vs = [vc_ref[:, j * LANES:(j + 1) * LANES], v_ref[:, j * LANES:(j + 1) * LANES]]
        res = []
        for e in range(2):
            c0 = (2 * j + e) * LANES
            q = q_ref[:, c0:c0 + LANES]
            s_ctx = _dot_nt(q, kc_ref[:, c0:c0 + LANES]) * B_SCALE
            s_lat = _dot_nt(q, k_ref[:, c0:c0 + LANES]) * B_SCALE
            res.append(_softmax_pv([s_ctx, s_lat], vs, None))
        o_ref[:, j * LANES:(j + 1) * LANES] = jnp.where(lo, res[0], res[1]).astype(BF16)


def _attn_b_lat_call(qb, kb, vb, kc, vc):
    tq = 256
    nq = DEC_SEQ // tq
    return pl.pallas_call(
        _attn_b_lat_kernel,
        grid=(DEC_BATCH, nq),
        in_specs=[
            pl.BlockSpec((tq, 1024), lambda b, n: (b * nq + n, 0)),
            pl.BlockSpec((DEC_SEQ, 1024), lambda b, n: (b, 0)),
            pl.BlockSpec((DEC_SEQ, 512), lambda b, n: (b, 0)),
            pl.BlockSpec((PAST, 1024), lambda b, n: (b, 0)),
            pl.BlockSpec((PAST, 512), lambda b, n: (b, 0)),
        ],
        out_specs=pl.BlockSpec((tq, 512), lambda b, n: (b * nq + n, 0)),
        out_shape=jax.ShapeDtypeStruct((NS, 512), BF16),
        compiler_params=_cparams("arbitrary", "arbitrary"),
        name="attn_b_latent",
    )(qb, kb, vb, kc, vc)


def _attn_ctx_kernel(sink_ref, qa_ref, ka_ref, va_ref, qb_ref, kb_ref, vb_ref, oa_ref, ob_ref):
    lo = _lane_lo()
    top = lax.broadcasted_iota(jnp.int32, (2 * SEQ, 1), 0) < SEQ
    for hk in range(A_KV_HEADS):
        q2 = jnp.concatenate([qa_ref[:, (2 * hk) * LANES:(2 * hk + 1) * LANES],
                              qa_ref[:, (2 * hk + 1) * LANES:(2 * hk + 2) * LANES]], axis=0)
        v = va_ref[:, hk * LANES:(hk + 1) * LANES]
        res = []
        for e in range(2):
            c0 = (2 * hk + e) * LANES
            s = _dot_nt(q2, ka_ref[:, c0:c0 + LANES])
            sink = jnp.where(top, sink_ref[4 * hk + e], sink_ref[4 * hk + 2 + e])
            res.append(_softmax_pv([s], [v], sink))
        o2 = jnp.where(lo, res[0], res[1]).astype(BF16)
        oa_ref[:, (2 * hk) * LANES:(2 * hk + 1) * LANES] = o2[:SEQ]
        oa_ref[:, (2 * hk + 1) * LANES:(2 * hk + 2) * LANES] = o2[SEQ:]
    for j in range(B_HEADS // 2):
        v = vb_ref[:, j * LANES:(j + 1) * LANES]
        res = []
        for e in range(2):
            c0 = (2 * j + e) * LANES
            s = _dot_nt(qb_ref[:, c0:c0 + LANES], kb_ref[:, c0:c0 + LANES]) * B_SCALE
            res.append(_softmax_pv([s], [v], None))
        ob_ref[:, j * LANES:(j + 1) * LANES] = jnp.where(lo, res[0], res[1]).astype(BF16)


def _attn_ctx_call(sink, qa, ka, va, qb, kb, vb):
    off = NS // SEQ

    def tok(w):
        return pl.BlockSpec((SEQ, w), lambda b: (off + b, 0))

    def out(w):
        return pl.BlockSpec((SEQ, w), lambda b: (b, 0))

    return pl.pallas_call(
        _attn_ctx_kernel,
        grid=(BATCH,),
        in_specs=[pl.BlockSpec(memory_space=pltpu.SMEM), tok(512), tok(512), tok(256), tok(1024), tok(1024), tok(512)],
        out_specs=[out(512), out(512)],
        out_shape=[jax.ShapeDtypeStruct((NP, 512), BF16), jax.ShapeDtypeStruct((NP, 512), BF16)],
        compiler_params=_cparams("arbitrary"),
        name="attn_context",
    )(sink, qa, ka, va, qb, kb, vb)


def _outproj_kernel(nparts, ns_tiles, *refs):
    y_ref, gt_ref = refs[0], refs[1]
    xs = refs[2:2 + 2 * nparts]
    ws = refs[2 + 2 * nparts:2 + 3 * nparts]
    g_ref, b_ref, o_ref = refs[2 + 3 * nparts:]
    latent = pl.program_id(0) < ns_tiles
    acc = None
    for k in range(nparts):
        x = jnp.where(latent, xs[2 * k][...], xs[2 * k + 1][...])
        t = _dot(x, ws[k][...])
        acc = t if acc is None else acc + t
    z = ALPHA * y_ref[...] + gt_ref[...] * acc
    o_ref[...] = _layer_norm(z, g_ref[...], b_ref[...])


def _outproj_call(y, modsr, lng, lnb, l, parts, tm=512):
    ns_tiles = NS // tm
    in_specs = [pl.BlockSpec((tm, D), lambda i: (i, 0)), _mod_spec(l, 5, tm, 1)]
    args = [y, modsr]
    for xs, xp, _, _ in parts:
        w = xs.shape[1]
        in_specs.append(pl.BlockSpec((tm, w), lambda i: (jnp.minimum(i, ns_tiles - 1), 0)))
        in_specs.append(pl.BlockSpec((tm, w), lambda i: (jnp.maximum(i - ns_tiles, 0), 0)))
        args += [xs, xp]
    for _, _, warr, wspec in parts:
        in_specs.append(wspec)
        args.append(warr)
    in_specs += [_ln_spec(l, 1), _ln_spec(l, 1)]
    args += [lng, lnb]
    return pl.pallas_call(
        functools.partial(_outproj_kernel, len(parts), ns_tiles),
        grid=(NTOK // tm,),
        in_specs=in_specs,
        out_specs=pl.BlockSpec((tm, D), lambda i: (i, 0)),
        out_shape=jax.ShapeDtypeStruct((NTOK, D), F32),
        compiler_params=_cparams("arbitrary"),
        name="mixer_out",
    )(*args)


def _fourier_in_kernel(x_ref, sh_ref, sc_ref, w_ref, cs_ref, p_ref):
    h = (x_ref[...] * (1.0 + sc_ref[...]) + sh_ref[...]).astype(BF16)
    u = _dot(h, w_ref[...]).astype(BF16)
    for g in range(F_GROUPS):
        pcs = _dot(u[:, g * F_GROUP_DIM:(g + 1) * F_GROUP_DIM], cs_ref[...]).astype(BF16)
        p_ref[:, g * F_GROUP_DIM:(g + 1) * F_GROUP_DIM] = pcs[:, :F_GROUP_DIM]
        p_ref[:, D + g * F_GROUP_DIM:D + (g + 1) * F_GROUP_DIM] = pcs[:, F_GROUP_DIM:]


def _fourier_in_call(y, modsr, l, w_in, cs, tm=512):
    return pl.pallas_call(
        _fourier_in_kernel,
        grid=(NTOK // tm,),
        in_specs=[
            pl.BlockSpec((tm, D), lambda i: (i, 0)), _mod_spec(l, 3, tm, 1), _mod_spec(l, 4, tm, 1),
            pl.BlockSpec(w_in.shape, lambda i: (0, 0)), pl.BlockSpec(cs.shape, lambda i: (0, 0)),
        ],
        out_specs=pl.BlockSpec((tm, 2 * D), lambda i: (i, 0)),
        out_shape=jax.ShapeDtypeStruct((NTOK, 2 * D), BF16),
        compiler_params=_cparams("arbitrary"),
        name="fourier_in",
    )(y, modsr, modsr, w_in, cs)


def _fourier_pos_kernel(norm, c_ref, s_ref, p_ref, f_ref):
    f = _dot(c_ref[...], p_ref[:, :D]) + _dot(s_ref[...], p_ref[:, D:])
    f_ref[...] = (f * norm).astype(BF16)


def _fourier_pos_call(p, cmat, smat, length, nbatch, row0):
    tk = min(length, 256)
    nk = length // tk
    off = row0 // length
    norm = float((length * F_GROUP_DIM) ** -0.5)
    return pl.pallas_call(
        functools.partial(_fourier_pos_kernel, norm),
        grid=(nbatch, nk),
        in_specs=[
            pl.BlockSpec((tk, length), lambda b, k: (k, 0)),
            pl.BlockSpec((tk, length), lambda b, k: (k, 0)),
            pl.BlockSpec((length, 2 * D), lambda b, k: (off + b, 0)),
        ],
        out_specs=pl.BlockSpec((tk, D), lambda b, k: (b * nk + k, 0)),
        out_shape=jax.ShapeDtypeStruct((nbatch * length, D), BF16),
        compiler_params=_cparams("arbitrary", "arbitrary"),
        name="fourier_pos",
    )(cmat, smat, p)


def kernel(x_prompt, x_sample, cache_a_k, cache_a_v, cache_b_ckv, cache_b_kpe, c, c_ctx, w_ada, b_ada, ln_g, ln_b,
           ffn_w_gate, ffn_w_up, ffn_w_down, ab_w_in, ab_w_out, a_sink, b_g_cq, b_w_uq, b_g_ckv, b_w_ukv,
           f_w_in, f_w_out):
    tm_in = 512
    y = jnp.concatenate([x_sample.reshape(NS, D), x_prompt.reshape(NP, D)], axis=0)
    cond = jnp.concatenate([c, c_ctx[None, :], jnp.zeros((MOD_ROWS - DEC_BATCH - 1, D), F32)], axis=0)
    lng = ln_g.reshape(DEPTH * 3, 1, D)
    lnb = ln_b.reshape(DEPTH * 3, 1, D)
    wg = ffn_w_gate.astype(BF16)
    wu = ffn_w_up.astype(BF16)
    wd = ffn_w_down.astype(BF16)
    w_in = jnp.pad(ab_w_in, ((0, 0), (0, 0), (0, AB_IN_PAD - AB_IN))).astype(BF16)
    w_out = ab_w_out.astype(BF16)
    w_uq = jnp.pad(b_w_uq.reshape(N_AB, B_Q_LORA, B_HEADS, B_NOPE + B_ROPE),
                   ((0, 0), (0, 0), (0, 0), (0, LANES - B_NOPE - B_ROPE))).reshape(N_AB, B_Q_LORA, 1024).astype(BF16)
    ukv = b_w_ukv.reshape(N_AB, B_KV_LORA, B_HEADS, B_NOPE + B_V)
    w_uk = jnp.pad(ukv[..., :B_NOPE], ((0, 0), (0, 0), (0, 0), (0, LANES - B_NOPE))).reshape(N_AB, B_KV_LORA, 1024)
    place = np.zeros((LANES, B_HEADS, LANES), np.float32)
    for d in range(B_ROPE):
        place[d, :, B_NOPE + d] = 1.0
    w_k = jnp.concatenate(
        [w_uk, jnp.broadcast_to(jnp.asarray(place.reshape(LANES, 1024)), (N_AB, LANES, 1024))], axis=1).astype(BF16)
    w_v = ukv[..., B_NOPE:].reshape(N_AB, B_KV_LORA, 512).astype(BF16)
    g_cq = b_g_cq.reshape(N_AB, 1, B_Q_LORA)
    g_ckv = b_g_ckv.reshape(N_AB, 1, B_KV_LORA)
    fw_in = f_w_in.astype(BF16)
    fw_out = f_w_out.astype(BF16)
    cache_k = cache_a_k.reshape(DEC_BATCH, N_AB, PAST, A_KV_HEADS * A_HEAD_DIM)
    cache_v = cache_a_v.reshape(DEC_BATCH, N_AB, PAST, A_KV_HEADS * A_HEAD_DIM)
    rope_tabs = jnp.asarray(_rope_tables(tm_in))
    cc, sc = _dft_mats(F_GROUP_DIM)
    cs_chan = jnp.asarray(np.concatenate([cc, sc], axis=1), dtype=BF16)
    cl, sl = _dft_mats(DEC_SEQ)
    cp, sp = _dft_mats(SEQ)
    dft_lat = (jnp.asarray(cl, dtype=BF16), jnp.asarray(-sl, dtype=BF16))
    dft_ctx = (jnp.asarray(cp, dtype=BF16), jnp.asarray(-sp, dtype=BF16))

    mods = _mods_call(cond, w_ada, b_ada)
    modsr = mods.reshape(DEPTH * MOD_ROWS * N_MODS, 1, D)

    new_k, new_v, new_ckv, new_kpe = [], [], [], []
    for l in range(DEPTH):
        li = l // 2
        y = _ffn_call(y, modsr, lng, lnb, wg, wu, wd, l, 0)
        if l % 2 == 0:
            (qa, ka, va, qb, kb, vb, cka, cva, cckv, ckpe) = _inproj_call(
                y, modsr, l, w_in[li], g_cq[li], w_uq[li], g_ckv[li], w_k[li], w_v[li], rope_tabs, tm_in)
            kc, vc = _ctxkv_call(cache_b_ckv, cache_b_kpe, li, w_k[li], w_v[li])
            oa_s = _attn_a_lat_call(a_sink[li], qa, ka, va, cache_k, cache_v, li)
            ob_s = _attn_b_lat_call(qb, kb, vb, kc, vc)
            oa_p, ob_p = _attn_ctx_call(a_sink[li], qa, ka, va, qb, kb, vb)
            parts = [
                (oa_s, oa_p, w_out, pl.BlockSpec((None, 512, D), lambda i, li=li: (li, 0, 0))),
                (ob_s, ob_p, w_out, pl.BlockSpec((None, 512, D), lambda i, li=li: (li, 1, 0))),
            ]
            y = _outproj_call(y, modsr, lng, lnb, l, parts)
            new_k.append(cka.reshape(BATCH, SEQ, A_KV_HEADS, A_HEAD_DIM))
            new_v.append(cva.reshape(BATCH, SEQ, A_KV_HEADS, A_HEAD_DIM))
            new_ckv.append(cckv.reshape(BATCH, SEQ, B_KV_LORA))
            new_kpe.append(ckpe.reshape(BATCH, SEQ, B_ROPE))
        else:
            p = _fourier_in_call(y, modsr, l, fw_in[li], cs_chan)
            f_s = _fourier_pos_call(p, dft_lat[0], dft_lat[1], DEC_SEQ, DEC_BATCH, 0)
            f_p = _fourier_pos_call(p, dft_ctx[0], dft_ctx[1], SEQ, BATCH, NS)
            parts = [(f_s, f_p, fw_out, pl.BlockSpec((None, D, D), lambda i, li=li: (li, 0, 0)))]
            y = _outproj_call(y, modsr, lng, lnb, l, parts)
        y = _ffn_call(y, modsr, lng, lnb, wg, wu, wd, l, 1)

    y_sample = y[:NS].reshape(DEC_BATCH, DEC_SEQ, D)
    y_prompt = y[NS:].reshape(BATCH, SEQ, D)
    return (y_prompt, y_sample, jnp.stack(new_k, axis=1), jnp.stack(new_v, axis=1),
            jnp.stack(new_ckv, axis=1), jnp.stack(new_kpe, axis=1))
```

```python
import functools

import numpy as np
import jax
import jax.numpy as jnp
from jax import lax
from jax.experimental import pallas as pl
from jax.experimental.pallas import tpu as pltpu

F32 = jnp.float32
BF16 = jnp.bfloat16

D = 1024
BATCH = 16
SEQ = 256
DEPTH = 4
DEC_BATCH = 8
DEC_SEQ = 2048
PAST = 256
GRID_W = 64
N_AB = (DEPTH + 1) // 2
A_HEADS = 8
A_KV_HEADS = 2
A_HEAD_DIM = 64
WINDOW = 128
BAND = 128
B_HEADS = 8
B_NOPE = 64
B_ROPE = 32
B_V = 64
B_Q_LORA = 384
B_KV_LORA = 256
AB_IN = 1440
AB_IN_PAD = 1536
A_SCALE = A_HEAD_DIM ** -0.5
B_SCALE = (B_NOPE + B_ROPE) ** -0.5
F_GROUPS = 4
F_GROUP_DIM = D // F_GROUPS
D_FF = 2816
FFN_HALF = 0.5
ALPHA = (2 * DEPTH) ** 0.25
LN_EPS = 1e-5
RMS_EPS = 1e-6
N_MODS = 9
ROPE_BASE = 10000.0
NEG_INF = -1e30
LOG2E = 1.4426950408889634

NS = DEC_BATCH * DEC_SEQ
NP = BATCH * SEQ
NTOK = NS + NP
MOD_ROWS = 16
CTX_ROW = DEC_BATCH
LANES = 128
VMEM_LIMIT = 56 * 1024 * 1024


def _cparams(*sem):
    return pltpu.CompilerParams(dimension_semantics=sem, vmem_limit_bytes=VMEM_LIMIT)


def _silu(x):
    return x * (1.0 / (1.0 + jnp.exp(-x)))


def _layer_norm(z, g, b):
    mu = jnp.mean(z, axis=-1, keepdims=True)
    zc = z - mu
    var = jnp.mean(zc * zc, axis=-1, keepdims=True)
    return zc * lax.rsqrt(var + LN_EPS) * g + b


def _rms_norm(x, g):
    return x * lax.rsqrt(jnp.mean(x * x, axis=-1, keepdims=True) + RMS_EPS) * g


def _dot(a, b):
    return jnp.dot(a, b, preferred_element_type=F32)


def _dot_nt(a, b):
    return lax.dot_general(a, b, (((1,), (1,)), ((), ())), preferred_element_type=F32)


def _rope_tables(tm):
    pos = np.arange(DEC_SEQ)
    row = (pos // GRID_W).astype(np.float64)
    col = (pos % GRID_W).astype(np.float64)

    def layout(lane_d, quarter):
        cos = np.ones((DEC_SEQ, LANES))
        sin_m = np.zeros((DEC_SEQ, LANES))
        sin_p = np.zeros((DEC_SEQ, LANES))
        for lane in range(LANES):
            d = lane_d[lane]
            if d < 0:
                continue
            q, j = d // quarter, d % quarter
            inv = ROPE_BASE ** (-float(j) / quarter)
            ang = (row if q < 2 else col) * inv
            cos[:, lane] = np.cos(ang)
            if q % 2 == 0:
                sin_m[:, lane] = -np.sin(ang)
            else:
                sin_p[:, lane] = np.sin(ang)
        return [cos, sin_m, sin_p]

    lane_a = [lane % A_HEAD_DIM for lane in range(LANES)]
    lane_b = [lane - B_NOPE if B_NOPE <= lane < B_NOPE + B_ROPE else -1 for lane in range(LANES)]
    lane_k = [lane if lane < B_ROPE else -1 for lane in range(LANES)]
    tabs = layout(lane_a, A_HEAD_DIM // 4) + layout(lane_b, B_ROPE // 4) + layout(lane_k, B_ROPE // 4)
    out = np.zeros((9, DEC_SEQ + tm, LANES), np.float32)
    for t, tab in enumerate(tabs):
        out[t, :DEC_SEQ] = tab
        out[t, DEC_SEQ:] = 1.0 if t % 3 == 0 else 0.0
    return out


def _dft_mats(n):
    k = np.arange(n)
    kn = (k[:, None] * k[None, :]) % n
    ang = 2.0 * np.pi * kn.astype(np.float64) / n
    return np.cos(ang), -np.sin(ang)


def _mods_kernel(c_ref, w_ref, b_ref, o_ref):
    x = _silu(c_ref[...]).astype(BF16)
    o_ref[...] = _dot(x, w_ref[...].astype(BF16)) + b_ref[...]


def _mods_call(cond, w_ada, b_ada):
    tn = 1024
    nw = N_MODS * D
    return pl.pallas_call(
        _mods_kernel,
        grid=(DEPTH, nw // tn),
        in_specs=[
            pl.BlockSpec((MOD_ROWS, D), lambda l, j: (0, 0)),
            pl.BlockSpec((None, D, tn), lambda l, j: (l, 0, j)),
            pl.BlockSpec((None, 1, tn), lambda l, j: (l, 0, j)),
        ],
        out_specs=pl.BlockSpec((None, MOD_ROWS, tn), lambda l, j: (l, 0, j)),
        out_shape=jax.ShapeDtypeStruct((DEPTH, MOD_ROWS, nw), F32),
        compiler_params=_cparams("arbitrary", "arbitrary"),
        name="ada_mods",
    )(cond, w_ada, b_ada.reshape(DEPTH, 1, nw))


def _mod_spec(l, m, tm, ngrid):
    def idx(*g):
        r = jnp.minimum((g[0] * tm) // DEC_SEQ, CTX_ROW)
        return (l * MOD_ROWS * N_MODS + r * N_MODS + m, 0, 0)
    del ngrid
    return pl.BlockSpec((None, 1, D), idx)


def _ln_spec(l, which):
    return pl.BlockSpec((None, 1, D), lambda *g: (l * 3 + which, 0, 0))


FFN_TM = 512
FFN_TF = 256


def _ffn_kernel(ns_tiles, split_in, split_out, *refs):
    n_in = 2 if split_in else 1
    n_out = 2 if split_out else 1
    x_refs = refs[:n_in]
    sh_ref, sc_ref, gt_ref, wg_ref, wu_ref, wd_ref, g_ref, b_ref = refs[n_in:n_in + 8]
    o_refs = refs[n_in + 8:n_in + 8 + n_out]
    h_scr, a_scr = refs[n_in + 8 + n_out:]
    latent = pl.program_id(0) < ns_tiles

    def load_x():
        if split_in:
            return jnp.where(latent, x_refs[0][...], x_refs[1][...])
        return x_refs[0][...]

    h_scr[...] = (load_x() * (1.0 + sc_ref[...]) + sh_ref[...]).astype(BF16)
    for c in range(D_FF // FFN_TF):
        cols = slice(c * FFN_TF, (c + 1) * FFN_TF)
        h = h_scr[...]
        a_scr[:, cols] = (_silu(_dot(h, wg_ref[:, cols])) * _dot(h, wu_ref[:, cols])).astype(BF16)
    down = _dot(a_scr[...], wd_ref[...])
    z = ALPHA * load_x() + gt_ref[...] * (FFN_HALF * down)
    out = _layer_norm(z, g_ref[...], b_ref[...])
    if split_out:
        @pl.when(latent)
        def _():
            o_refs[0][...] = out

        @pl.when(jnp.logical_not(latent))
        def _():
            o_refs[1][...] = out
    else:
        o_refs[0][...] = out


def _stream_specs(tm, ns_tiles):
    return [pl.BlockSpec((tm, D), lambda i: (jnp.minimum(i, ns_tiles - 1), 0)),
            pl.BlockSpec((tm, D), lambda i: (jnp.maximum(i - ns_tiles, 0), 0))]


def _ffn_call(ys, modsr, lng, lnb, wg, wu, wd, l, k, split_out=False):
    tm = FFN_TM
    ns_tiles = NS // tm
    split_in = isinstance(ys, tuple)
    m0 = 0 if k == 0 else 6
    which = 0 if k == 0 else 2
    joint = pl.BlockSpec((tm, D), lambda i: (i, 0))
    resident = pl.Buffered(1)
    x_specs = _stream_specs(tm, ns_tiles) if split_in else [joint]
    if split_out:
        out_specs = _stream_specs(tm, ns_tiles)
        out_shape = [jax.ShapeDtypeStruct((NS, D), F32), jax.ShapeDtypeStruct((NP, D), F32)]
    else:
        out_specs = joint
        out_shape = jax.ShapeDtypeStruct((NTOK, D), F32)
    return pl.pallas_call(
        functools.partial(_ffn_kernel, ns_tiles, split_in, split_out),
        grid=(NTOK // tm,),
        in_specs=x_specs + [
            _mod_spec(l, m0, tm, 1), _mod_spec(l, m0 + 1, tm, 1), _mod_spec(l, m0 + 2, tm, 1),
            pl.BlockSpec((None, None, D, D_FF), lambda i: (l, k, 0, 0), pipeline_mode=resident),
            pl.BlockSpec((None, None, D, D_FF), lambda i: (l, k, 0, 0), pipeline_mode=resident),
            pl.BlockSpec((None, None, D_FF, D), lambda i: (l, k, 0, 0), pipeline_mode=resident),
            _ln_spec(l, which), _ln_spec(l, which),
        ],
        out_specs=out_specs,
        out_shape=out_shape,
        scratch_shapes=[pltpu.VMEM((tm, D), BF16), pltpu.VMEM((tm, D_FF), BF16)],
        compiler_params=_cparams("arbitrary"),
        name="half_ffn",
    )(*(ys if split_in else (ys,)), modsr, modsr, modsr, wg, wu, wd, lng, lnb)


def _rope(x, cos, sin_m, sin_p, quarter):
    n = x.shape[1] // LANES
    up = pltpu.roll(x, x.shape[1] - quarter, 1)
    dn = pltpu.roll(x, quarter, 1)
    if n > 1:
        cos = jnp.concatenate([cos] * n, axis=1)
        sin_m = jnp.concatenate([sin_m] * n, axis=1)
        sin_p = jnp.concatenate([sin_p] * n, axis=1)
    return x * cos + up * sin_m + dn * sin_p


def _inproj_kernel(ns_tiles, x_ref, sh_ref, sc_ref, w_in_ref, gq_ref, wq_ref, gkv_ref, wk_ref, wv_ref, rope_ref,
                   qa_ref, ka_ref, va_ref, qb_ref, kb_ref, vb_ref,
                   cka_ref, cva_ref, cckv_ref, ckpe_ref):
    i = pl.program_id(0)
    h = (x_ref[...] * (1.0 + sc_ref[...]) + sh_ref[...]).astype(BF16)
    proj = _dot(h, w_in_ref[...])
    qa = proj[:, 0:512]
    ka = proj[:, 512:640]
    va = proj[:, 640:768]
    cq = proj[:, 768:1152]
    ckv = proj[:, 1152:1408]
    kpe = proj[:, 1408:1536]
    ckv_n = _rms_norm(ckv, gkv_ref[...])

    @pl.when(i >= ns_tiles)
    def _():
        cka_ref[...] = ka
        cva_ref[...] = va
        cckv_ref[...] = ckv_n
        ckpe_ref[...] = kpe[:, :B_ROPE]

    lane = lax.broadcasted_iota(jnp.int32, (1, LANES), 1)
    lo = lane < A_HEAD_DIM

    qa_ref[...] = (_rope(qa, rope_ref[0], rope_ref[1], rope_ref[2], A_HEAD_DIM // 4) * (A_SCALE * LOG2E)).astype(BF16)
    ka_r = _rope(ka, rope_ref[0], rope_ref[1], rope_ref[2], A_HEAD_DIM // 4)
    ka_s = pltpu.roll(ka_r, A_HEAD_DIM, 1)
    zero = jnp.zeros_like(ka_r)
    ka_ref[...] = jnp.concatenate(
        [jnp.where(lo, ka_r, zero), jnp.where(lo, zero, ka_s), jnp.where(lo, ka_s, zero), jnp.where(lo, zero, ka_r)],
        axis=1).astype(BF16)
    va_s = pltpu.roll(va, A_HEAD_DIM, 1)
    va_ref[...] = jnp.concatenate([jnp.where(lo, va, va_s), jnp.where(lo, va_s, va)], axis=1).astype(BF16)

    cq_n = _rms_norm(cq, gq_ref[...]).astype(BF16)
    qb = _dot(cq_n, wq_ref[...])
    qb_ref[...] = (_rope(qb, rope_ref[3], rope_ref[4], rope_ref[5], B_ROPE // 4) * (B_SCALE * LOG2E)).astype(BF16)
    kpe_r = _rope(kpe, rope_ref[6], rope_ref[7], rope_ref[8], B_ROPE // 4)
    lat = jnp.concatenate([ckv_n.astype(BF16), kpe_r.astype(BF16)], axis=1)
    kb_ref[...] = _dot(lat, wk_ref[...]).astype(BF16)
    vb_ref[...] = _dot(lat[:, :B_KV_LORA], wv_ref[...]).astype(BF16)


def _inproj_call(y, modsr, l, w_in, g_cq, w_uq, g_ckv, w_k, w_v, rope_tabs, tm):
    ns_tiles = NS // tm
    pos_tiles = DEC_SEQ // tm

    def tok(w):
        return pl.BlockSpec((tm, w), lambda i: (i, 0))

    def ctx(w):
        return pl.BlockSpec((tm, w), lambda i: (jnp.maximum(i - ns_tiles, 0), 0))

    def full(a):
        return pl.BlockSpec(a.shape, lambda i: (0,) * a.ndim)

    def rope_idx(i):
        return (0, jnp.where(i < ns_tiles, i % pos_tiles, pos_tiles), 0)

    outs = pl.pallas_call(
        functools.partial(_inproj_kernel, ns_tiles),
        grid=(NTOK // tm,),
        in_specs=[
            tok(D), _mod_spec(l, 3, tm, 1), _mod_spec(l, 4, tm, 1),
            full(w_in), full(g_cq), full(w_uq), full(g_ckv), full(w_k), full(w_v),
            pl.BlockSpec((9, tm, LANES), rope_idx),
        ],
        out_specs=[tok(512), tok(512), tok(256), tok(1024), tok(1024), tok(512),
                   ctx(128), ctx(128), ctx(B_KV_LORA), ctx(B_ROPE)],
        out_shape=[
            jax.ShapeDtypeStruct((NTOK, 512), BF16), jax.ShapeDtypeStruct((NTOK, 512), BF16),
            jax.ShapeDtypeStruct((NTOK, 256), BF16), jax.ShapeDtypeStruct((NTOK, 1024), BF16),
            jax.ShapeDtypeStruct((NTOK, 1024), BF16), jax.ShapeDtypeStruct((NTOK, 512), BF16),
            jax.ShapeDtypeStruct((NP, 128), F32), jax.ShapeDtypeStruct((NP, 128), F32),
            jax.ShapeDtypeStruct((NP, B_KV_LORA), F32), jax.ShapeDtypeStruct((NP, B_ROPE), F32),
        ],
        compiler_params=_cparams("arbitrary"),
        name="ab_inproj",
    )(y, modsr, modsr, w_in, g_cq, w_uq, g_ckv, w_k, w_v, rope_tabs)
    return outs


def _ctxkv_kernel(ckv_ref, kpe_ref, wk_ref, wv_ref, k_ref, v_ref):
    ckv = ckv_ref[...].astype(BF16)
    k = _dot(ckv, wk_ref[:B_KV_LORA, :]) + _dot(kpe_ref[...].astype(BF16), wk_ref[B_KV_LORA:B_KV_LORA + B_ROPE, :])
    k_ref[...] = k.astype(BF16)
    v_ref[...] = _dot(ckv, wv_ref[...]).astype(BF16)


def _ctxkv_call(cache_ckv, cache_kpe, li, w_k, w_v):
    return pl.pallas_call(
        _ctxkv_kernel,
        grid=(DEC_BATCH,),
        in_specs=[
            pl.BlockSpec((None, None, PAST, B_KV_LORA), lambda b: (b, li, 0, 0)),
            pl.BlockSpec((None, None, PAST, B_ROPE), lambda b: (b, li, 0, 0)),
            pl.BlockSpec(w_k.shape, lambda b: (0, 0)),
            pl.BlockSpec(w_v.shape, lambda b: (0, 0)),
        ],
        out_specs=[pl.BlockSpec((PAST, 1024), lambda b: (b, 0)), pl.BlockSpec((PAST, 512), lambda b: (b, 0))],
        out_shape=[jax.ShapeDtypeStruct((DEC_BATCH * PAST, 1024), BF16),
                   jax.ShapeDtypeStruct((DEC_BATCH * PAST, 512), BF16)],
        compiler_params=_cparams("arbitrary"),
        name="b_ctx_kv",
    )(cache_ckv, cache_kpe, w_k, w_v)


def _softmax_pv(s, values, sink):
    m = jnp.max(s, axis=-1, keepdims=True)
    if sink is not None:
        m = jnp.maximum(m, sink)
    p = jnp.exp2(s - m)
    den = jnp.sum(p, axis=-1, keepdims=True)
    if sink is not None:
        den = den + jnp.exp2(sink - m)
    p = p.astype(BF16)
    out, c0 = None, 0
    for v in values:
        o = _dot(p[:, c0:c0 + v.shape[0]], v)
        c0 += v.shape[0]
        out = o if out is None else out + o
    return out * (1.0 / den)


def _lane_lo():
    return lax.broadcasted_iota(jnp.int32, (1, LANES), 1) < A_HEAD_DIM


A_LAT_BLOCKS = 2


def _attn_a_lat_kernel(sink_ref, q_ref, k_ref, v_ref, kc_ref, vc_ref, o_ref):
    nb = DEC_SEQ // BAND
    lo = _lane_lo()
    kc = kc_ref[...]
    vc = vc_ref[...]
    kc_s = pltpu.roll(kc, A_HEAD_DIM, 1)
    vc_s = pltpu.roll(vc, A_HEAD_DIM, 1)
    zero = jnp.zeros_like(kc)
    kc_var = [jnp.where(lo, kc, zero), jnp.where(lo, zero, kc_s), jnp.where(lo, kc_s, zero), jnp.where(lo, zero, kc)]
    kc_var = [k.astype(BF16) for k in kc_var]
    vc_dup = [jnp.where(lo, vc, vc_s).astype(BF16), jnp.where(lo, vc_s, vc).astype(BF16)]

    nkeys = PAST + 3 * BAND
    col = lax.broadcasted_iota(jnp.int32, (2 * BAND, nkeys), 1)
    qi = lax.broadcasted_iota(jnp.int32, (2 * BAND, nkeys), 0) % BAND
    t = col - PAST
    top = lax.broadcasted_iota(jnp.int32, (2 * BAND, 1), 0) < BAND

    for sub in range(A_LAT_BLOCKS):
        n = pl.program_id(1) * A_LAT_BLOCKS + sub
        rows = [pl.multiple_of(jnp.maximum(n - 1, 0) * BAND, BAND), pl.multiple_of(n * BAND, BAND),
                pl.multiple_of(jnp.minimum(n + 1, nb - 1) * BAND, BAND)]
        t_min = jnp.where(n > 0, 0, BAND)
        t_max = jnp.where(n < nb - 1, 3 * BAND - 1, 2 * BAND - 1)
        lower = jnp.where(col < PAST, -PAST, jnp.maximum(qi + (BAND - WINDOW), t_min))
        upper = jnp.minimum(qi + (BAND + WINDOW), t_max)
        valid = (t >= lower) & (t <= upper)
        q0 = sub * BAND
        for hk in range(A_KV_HEADS):
            q2 = jnp.concatenate([q_ref[q0:q0 + BAND, (2 * hk) * LANES:(2 * hk + 1) * LANES],
                                  q_ref[q0:q0 + BAND, (2 * hk + 1) * LANES:(2 * hk + 2) * LANES]], axis=0)
            vs = jnp.concatenate(
                [vc_dup[hk]] + [v_ref[pl.ds(r, BAND), hk * LANES:(hk + 1) * LANES] for r in rows], axis=0)
            res = []
            for e in range(2):
                c0 = (2 * hk + e) * LANES
                ks = jnp.concatenate([kc_var[2 * hk + e]] + [k_ref[pl.ds(r, BAND), c0:c0 + LANES] for r in rows], axis=0)
                s = jnp.where(valid, _dot_nt(q2, ks), NEG_INF)
                sink = jnp.where(top, sink_ref[4 * hk + e], sink_ref[4 * hk + 2 + e]) * LOG2E
                res.append(_softmax_pv(s, [vs], sink))
            o2 = jnp.where(lo, res[0], res[1]).astype(BF16)
            o_ref[q0:q0 + BAND, (2 * hk) * LANES:(2 * hk + 1) * LANES] = o2[:BAND]
            o_ref[q0:q0 + BAND, (2 * hk + 1) * LANES:(2 * hk + 2) * LANES] = o2[BAND:]


def _attn_a_lat_call(sink, qa, ka, va, cache_k, cache_v, li):
    tq = A_LAT_BLOCKS * BAND
    nb = DEC_SEQ // tq
    return pl.pallas_call(
        _attn_a_lat_kernel,
        grid=(DEC_BATCH, nb),
        in_specs=[
            pl.BlockSpec(memory_space=pltpu.SMEM),
            pl.BlockSpec((tq, 512), lambda b, n: (b * nb + n, 0)),
            pl.BlockSpec((DEC_SEQ, 512), lambda b, n: (b, 0)),
            pl.BlockSpec((DEC_SEQ, 256), lambda b, n: (b, 0)),
            pl.BlockSpec((None, None, PAST, 128), lambda b, n: (b, li, 0, 0)),
            pl.BlockSpec((None, None, PAST, 128), lambda b, n: (b, li, 0, 0)),
        ],
        out_specs=pl.BlockSpec((tq, 512), lambda b, n: (b * nb + n, 0)),
        out_shape=jax.ShapeDtypeStruct((NS, 512), BF16),
        compiler_params=_cparams("arbitrary", "arbitrary"),
        name="attn_a_latent",
    )(sink, qa, ka, va, cache_k, cache_v)


def _attn_b_lat_kernel(q_ref, k_ref, v_ref, kc_ref, vc_ref, o_ref):
    lo = _lane_lo()
    for j in range(B_HEADS // 2):
        vs = [vc_ref[:, j * LANES:(j + 1) * LANES], v_ref[:, j * LANES:(j + 1) * LANES]]
        res = []
        for e in range(2):
            c0 = (2 * j + e) * LANES
            q = q_ref[:, c0:c0 + LANES]
            s = jnp.concatenate([_dot_nt(q, kc_ref[:, c0:c0 + LANES]), _dot_nt(q, k_ref[:, c0:c0 + LANES])], axis=1)
            res.append(_softmax_pv(s, vs, None))
        o_ref[:, j * LANES:(j + 1) * LANES] = jnp.where(lo, res[0], res[1]).astype(BF16)


def _attn_b_lat_call(qb, kb, vb, kc, vc):
    tq = 256
    nq = DEC_SEQ // tq
    return pl.pallas_call(
        _attn_b_lat_kernel,
        grid=(DEC_BATCH, nq),
        in_specs=[
            pl.BlockSpec((tq, 1024), lambda b, n: (b * nq + n, 0)),
            pl.BlockSpec((DEC_SEQ, 1024), lambda b, n: (b, 0)),
            pl.BlockSpec((DEC_SEQ, 512), lambda b, n: (b, 0)),
            pl.BlockSpec((PAST, 1024), lambda b, n: (b, 0)),
            pl.BlockSpec((PAST, 512), lambda b, n: (b, 0)),
        ],
        out_specs=pl.BlockSpec((tq, 512), lambda b, n: (b * nq + n, 0)),
        out_shape=jax.ShapeDtypeStruct((NS, 512), BF16),
        compiler_params=_cparams("arbitrary", "arbitrary"),
        name="attn_b_latent",
    )(qb, kb, vb, kc, vc)


def _attn_ctx_kernel(sink_ref, qa_ref, ka_ref, va_ref, qb_ref, kb_ref, vb_ref, oa_ref, ob_ref):
    lo = _lane_lo()
    top = lax.broadcasted_iota(jnp.int32, (2 * SEQ, 1), 0) < SEQ
    for hk in range(A_KV_HEADS):
        q2 = jnp.concatenate([qa_ref[:, (2 * hk) * LANES:(2 * hk + 1) * LANES],
                              qa_ref[:, (2 * hk + 1) * LANES:(2 * hk + 2) * LANES]], axis=0)
        v = va_ref[:, hk * LANES:(hk + 1) * LANES]
        res = []
        for e in range(2):
            c0 = (2 * hk + e) * LANES
            s = _dot_nt(q2, ka_ref[:, c0:c0 + LANES])
            sink = jnp.where(top, sink_ref[4 * hk + e], sink_ref[4 * hk + 2 + e]) * LOG2E
            res.append(_softmax_pv(s, [v], sink))
        o2 = jnp.where(lo, res[0], res[1]).astype(BF16)
        oa_ref[:, (2 * hk) * LANES:(2 * hk + 1) * LANES] = o2[:SEQ]
        oa_ref[:, (2 * hk + 1) * LANES:(2 * hk + 2) * LANES] = o2[SEQ:]
    for j in range(B_HEADS // 2):
        v = vb_ref[:, j * LANES:(j + 1) * LANES]
        res = []
        for e in range(2):
            c0 = (2 * j + e) * LANES
            s = _dot_nt(qb_ref[:, c0:c0 + LANES], kb_ref[:, c0:c0 + LANES])
            res.append(_softmax_pv(s, [v], None))
        ob_ref[:, j * LANES:(j + 1) * LANES] = jnp.where(lo, res[0], res[1]).astype(BF16)


def _attn_ctx_call(sink, qa, ka, va, qb, kb, vb):
    off = NS // SEQ

    def tok(w):
        return pl.BlockSpec((SEQ, w), lambda b: (off + b, 0))

    def out(w):
        return pl.BlockSpec((SEQ, w), lambda b: (b, 0))

    return pl.pallas_call(
        _attn_ctx_kernel,
        grid=(BATCH,),
        in_specs=[pl.BlockSpec(memory_space=pltpu.SMEM), tok(512), tok(512), tok(256), tok(1024), tok(1024), tok(512)],
        out_specs=[out(512), out(512)],
        out_shape=[jax.ShapeDtypeStruct((NP, 512), BF16), jax.ShapeDtypeStruct((NP, 512), BF16)],
        compiler_params=_cparams("arbitrary"),
        name="attn_context",
    )(sink, qa, ka, va, qb, kb, vb)


def _outproj_kernel(nparts, ns_tiles, *refs):
    y_ref, gt_ref = refs[0], refs[1]
    xs = refs[2:2 + 2 * nparts]
    ws = refs[2 + 2 * nparts:2 + 3 * nparts]
    g_ref, b_ref, o_ref = refs[2 + 3 * nparts:]
    latent = pl.program_id(0) < ns_tiles
    acc = None
    for k in range(nparts):
        x = jnp.where(latent, xs[2 * k][...], xs[2 * k + 1][...])
        t = _dot(x, ws[k][...])
        acc = t if acc is None else acc + t
    z = ALPHA * y_ref[...] + gt_ref[...] * acc
    o_ref[...] = _layer_norm(z, g_ref[...], b_ref[...])


def _outproj_call(y, modsr, lng, lnb, l, parts, tm=512):
    ns_tiles = NS // tm
    in_specs = [pl.BlockSpec((tm, D), lambda i: (i, 0)), _mod_spec(l, 5, tm, 1)]
    args = [y, modsr]
    for xs, xp, _, _ in parts:
        w = xs.shape[1]
        in_specs.append(pl.BlockSpec((tm, w), lambda i: (jnp.minimum(i, ns_tiles - 1), 0)))
        in_specs.append(pl.BlockSpec((tm, w), lambda i: (jnp.maximum(i - ns_tiles, 0), 0)))
        args += [xs, xp]
    for _, _, warr, wspec in parts:
        in_specs.append(wspec)
        args.append(warr)
    in_specs += [_ln_spec(l, 1), _ln_spec(l, 1)]
    args += [lng, lnb]
    return pl.pallas_call(
        functools.partial(_outproj_kernel, len(parts), ns_tiles),
        grid=(NTOK // tm,),
        in_specs=in_specs,
        out_specs=pl.BlockSpec((tm, D), lambda i: (i, 0)),
        out_shape=jax.ShapeDtypeStruct((NTOK, D), F32),
        compiler_params=_cparams("arbitrary"),
        name="mixer_out",
    )(*args)


def _fourier_in_kernel(x_ref, sh_ref, sc_ref, w_ref, cs_ref, p_ref):
    h = (x_ref[...] * (1.0 + sc_ref[...]) + sh_ref[...]).astype(BF16)
    u = _dot(h, w_ref[...]).astype(BF16)
    for g in range(F_GROUPS):
        pcs = _dot(u[:, g * F_GROUP_DIM:(g + 1) * F_GROUP_DIM], cs_ref[...]).astype(BF16)
        p_ref[:, g * F_GROUP_DIM:(g + 1) * F_GROUP_DIM] = pcs[:, :F_GROUP_DIM]
        p_ref[:, D + g * F_GROUP_DIM:D + (g + 1) * F_GROUP_DIM] = pcs[:, F_GROUP_DIM:]


def _fourier_in_call(y, modsr, l, w_in, cs, tm=512):
    return pl.pallas_call(
        _fourier_in_kernel,
        grid=(NTOK // tm,),
        in_specs=[
            pl.BlockSpec((tm, D), lambda i: (i, 0)), _mod_spec(l, 3, tm, 1), _mod_spec(l, 4, tm, 1),
            pl.BlockSpec(w_in.shape, lambda i: (0, 0)), pl.BlockSpec(cs.shape, lambda i: (0, 0)),
        ],
        out_specs=pl.BlockSpec((tm, 2 * D), lambda i: (i, 0)),
        out_shape=jax.ShapeDtypeStruct((NTOK, 2 * D), BF16),
        compiler_params=_cparams("arbitrary"),
        name="fourier_in",
    )(y, modsr, modsr, w_in, cs)


def _fourier_pos_kernel(norm, c_ref, s_ref, p_ref, f_ref):
    f = _dot(c_ref[...], p_ref[:, :D]) + _dot(s_ref[...], p_ref[:, D:])
    f_ref[...] = (f * norm).astype(BF16)


def _fourier_pos_call(p, cmat, smat, length, nbatch, row0):
    tk = min(length, 256)
    nk = length // tk
    off = row0 // length
    norm = float((length * F_GROUP_DIM) ** -0.5)
    return pl.pallas_call(
        functools.partial(_fourier_pos_kernel, norm),
        grid=(nbatch, nk),
        in_specs=[
            pl.BlockSpec((tk, length), lambda b, k: (k, 0)),
            pl.BlockSpec((tk, length), lambda b, k: (k, 0)),
            pl.BlockSpec((length, 2 * D), lambda b, k: (off + b, 0)),
        ],
        out_specs=pl.BlockSpec((tk, D), lambda b, k: (b * nk + k, 0)),
        out_shape=jax.ShapeDtypeStruct((nbatch * length, D), BF16),
        compiler_params=_cparams("arbitrary", "arbitrary"),
        name="fourier_pos",
    )(cmat, smat, p)


def kernel(x_prompt, x_sample, cache_a_k, cache_a_v, cache_b_ckv, cache_b_kpe, c, c_ctx, w_ada, b_ada, ln_g, ln_b,
           ffn_w_gate, ffn_w_up, ffn_w_down, ab_w_in, ab_w_out, a_sink, b_g_cq, b_w_uq, b_g_ckv, b_w_ukv,
           f_w_in, f_w_out):
    tm_in = 512
    y = (x_sample.reshape(NS, D), x_prompt.reshape(NP, D))
    cond = jnp.concatenate([c, c_ctx[None, :], jnp.zeros((MOD_ROWS - DEC_BATCH - 1, D), F32)], axis=0)
    lng = ln_g.reshape(DEPTH * 3, 1, D)
    lnb = ln_b.reshape(DEPTH * 3, 1, D)
    wg = ffn_w_gate.astype(BF16)
    wu = ffn_w_up.astype(BF16)
    wd = ffn_w_down.astype(BF16)
    w_in = jnp.pad(ab_w_in, ((0, 0), (0, 0), (0, AB_IN_PAD - AB_IN))).astype(BF16)
    w_out = ab_w_out.astype(BF16)
    w_uq = jnp.pad(b_w_uq.reshape(N_AB, B_Q_LORA, B_HEADS, B_NOPE + B_ROPE),
                   ((0, 0), (0, 0), (0, 0), (0, LANES - B_NOPE - B_ROPE))).reshape(N_AB, B_Q_LORA, 1024).astype(BF16)
    ukv = b_w_ukv.reshape(N_AB, B_KV_LORA, B_HEADS, B_NOPE + B_V)
    w_uk = jnp.pad(ukv[..., :B_NOPE], ((0, 0), (0, 0), (0, 0), (0, LANES - B_NOPE))).reshape(N_AB, B_KV_LORA, 1024)
    place = np.zeros((LANES, B_HEADS, LANES), np.float32)
    for d in range(B_ROPE):
        place[d, :, B_NOPE + d] = 1.0
    w_k = jnp.concatenate(
        [w_uk, jnp.broadcast_to(jnp.asarray(place.reshape(LANES, 1024)), (N_AB, LANES, 1024))], axis=1).astype(BF16)
    w_v = ukv[..., B_NOPE:].reshape(N_AB, B_KV_LORA, 512).astype(BF16)
    g_cq = b_g_cq.reshape(N_AB, 1, B_Q_LORA)
    g_ckv = b_g_ckv.reshape(N_AB, 1, B_KV_LORA)
    fw_in = f_w_in.astype(BF16)
    fw_out = f_w_out.astype(BF16)
    cache_k = cache_a_k.reshape(DEC_BATCH, N_AB, PAST, A_KV_HEADS * A_HEAD_DIM)
    cache_v = cache_a_v.reshape(DEC_BATCH, N_AB, PAST, A_KV_HEADS * A_HEAD_DIM)
    rope_tabs = jnp.asarray(_rope_tables(tm_in))
    cc, sc = _dft_mats(F_GROUP_DIM)
    cs_chan = jnp.asarray(np.concatenate([cc, sc], axis=1), dtype=BF16)
    cl, sl = _dft_mats(DEC_SEQ)
    cp, sp = _dft_mats(SEQ)
    dft_lat = (jnp.asarray(cl, dtype=BF16), jnp.asarray(-sl, dtype=BF16))
    dft_ctx = (jnp.asarray(cp, dtype=BF16), jnp.asarray(-sp, dtype=BF16))

    mods = _mods_call(cond, w_ada, b_ada)
    modsr = mods.reshape(DEPTH * MOD_ROWS * N_MODS, 1, D)

    new_k, new_v, new_ckv, new_kpe = [], [], [], []
    for l in range(DEPTH):
        li = l // 2
        y = _ffn_call(y, modsr, lng, lnb, wg, wu, wd, l, 0)
        if l % 2 == 0:
            (qa, ka, va, qb, kb, vb, cka, cva, cckv, ckpe) = _inproj_call(
                y, modsr, l, w_in[li], g_cq[li], w_uq[li], g_ckv[li], w_k[li], w_v[li], rope_tabs, tm_in)
            kc, vc = _ctxkv_call(cache_b_ckv, cache_b_kpe, li, w_k[li], w_v[li])
            oa_s = _attn_a_lat_call(a_sink[li], qa, ka, va, cache_k, cache_v, li)
            ob_s = _attn_b_lat_call(qb, kb, vb, kc, vc)
            oa_p, ob_p = _attn_ctx_call(a_sink[li], qa, ka, va, qb, kb, vb)
            parts = [
                (oa_s, oa_p, w_out, pl.BlockSpec((None, 512, D), lambda i, li=li: (li, 0, 0))),
                (ob_s, ob_p, w_out, pl.BlockSpec((None, 512, D), lambda i, li=li: (li, 1, 0))),
            ]
            y = _outproj_call(y, modsr, lng, lnb, l, parts)
            new_k.append(cka.reshape(BATCH, SEQ, A_KV_HEADS, A_HEAD_DIM))
            new_v.append(cva.reshape(BATCH, SEQ, A_KV_HEADS, A_HEAD_DIM))
            new_ckv.append(cckv.reshape(BATCH, SEQ, B_KV_LORA))
            new_kpe.append(ckpe.reshape(BATCH, SEQ, B_ROPE))
        else:
            p = _fourier_in_call(y, modsr, l, fw_in[li], cs_chan)
            f_s = _fourier_pos_call(p, dft_lat[0], dft_lat[1], DEC_SEQ, DEC_BATCH, 0)
            f_p = _fourier_pos_call(p, dft_ctx[0], dft_ctx[1], SEQ, BATCH, NS)
            parts = [(f_s, f_p, fw_out, pl.BlockSpec((None, D, D), lambda i, li=li: (li, 0, 0)))]
            y = _outproj_call(y, modsr, lng, lnb, l, parts)
        y = _ffn_call(y, modsr, lng, lnb, wg, wu, wd, l, 1, split_out=(l == DEPTH - 1))

    y_sample = y[0].reshape(DEC_BATCH, DEC_SEQ, D)
    y_prompt = y[1].reshape(BATCH, SEQ, D)
    return (y_prompt, y_sample, jnp.stack(new_k, axis=1), jnp.stack(new_v, axis=1),
            jnp.stack(new_ckv, axis=1), jnp.stack(new_kpe, axis=1))
```

```python
import functools

import numpy as np
import jax
import jax.numpy as jnp
from jax import lax
from jax.experimental import pallas as pl
from jax.experimental.pallas import tpu as pltpu

F32 = jnp.float32
BF16 = jnp.bfloat16

D = 1024
BATCH = 16
SEQ = 256
DEPTH = 4
DEC_BATCH = 8
DEC_SEQ = 2048
PAST = 256
GRID_W = 64
N_AB = (DEPTH + 1) // 2
A_HEADS = 8
A_KV_HEADS = 2
A_HEAD_DIM = 64
WINDOW = 128
BAND = 128
B_HEADS = 8
B_NOPE = 64
B_ROPE = 32
B_V = 64
B_Q_LORA = 384
B_KV_LORA = 256
AB_IN = 1440
AB_IN_PAD = 1536
A_SCALE = A_HEAD_DIM ** -0.5
B_SCALE = (B_NOPE + B_ROPE) ** -0.5
F_GROUPS = 4
F_GROUP_DIM = D // F_GROUPS
D_FF = 2816
FFN_HALF = 0.5
ALPHA = (2 * DEPTH) ** 0.25
LN_EPS = 1e-5
RMS_EPS = 1e-6
N_MODS = 9
ROPE_BASE = 10000.0
NEG_INF = -1e30
LOG2E = 1.4426950408889634

NS = DEC_BATCH * DEC_SEQ
NP = BATCH * SEQ
NTOK = NS + NP
MOD_ROWS = 16
CTX_ROW = DEC_BATCH
LANES = 128
VMEM_LIMIT = 56 * 1024 * 1024


def _cparams(*sem):
    return pltpu.CompilerParams(dimension_semantics=sem, vmem_limit_bytes=VMEM_LIMIT)


def _silu(x):
    return x * (1.0 / (1.0 + jnp.exp(-x)))


def _layer_norm(z, g, b):
    mu = jnp.mean(z, axis=-1, keepdims=True)
    zc = z - mu
    var = jnp.mean(zc * zc, axis=-1, keepdims=True)
    return zc * lax.rsqrt(var + LN_EPS) * g + b


def _rms_norm(x, g):
    return x * lax.rsqrt(jnp.mean(x * x, axis=-1, keepdims=True) + RMS_EPS) * g


def _dot(a, b):
    return jnp.dot(a, b, preferred_element_type=F32)


def _dot_nt(a, b):
    return lax.dot_general(a, b, (((1,), (1,)), ((), ())), preferred_element_type=F32)


def _rope_tables(tm):
    pos = np.arange(DEC_SEQ)
    row = (pos // GRID_W).astype(np.float64)
    col = (pos % GRID_W).astype(np.float64)

    def layout(lane_d, quarter):
        cos = np.ones((DEC_SEQ, LANES))
        sin_m = np.zeros((DEC_SEQ, LANES))
        sin_p = np.zeros((DEC_SEQ, LANES))
        for lane in range(LANES):
            d = lane_d[lane]
            if d < 0:
                continue
            q, j = d // quarter, d % quarter
            inv = ROPE_BASE ** (-float(j) / quarter)
            ang = (row if q < 2 else col) * inv
            cos[:, lane] = np.cos(ang)
            if q % 2 == 0:
                sin_m[:, lane] = -np.sin(ang)
            else:
                sin_p[:, lane] = np.sin(ang)
        return [cos, sin_m, sin_p]

    lane_a = [lane % A_HEAD_DIM for lane in range(LANES)]
    lane_b = [lane - B_NOPE if B_NOPE <= lane < B_NOPE + B_ROPE else -1 for lane in range(LANES)]
    lane_k = [lane if lane < B_ROPE else -1 for lane in range(LANES)]
    tabs = layout(lane_a, A_HEAD_DIM // 4) + layout(lane_b, B_ROPE // 4) + layout(lane_k, B_ROPE // 4)
    out = np.zeros((9, DEC_SEQ + tm, LANES), np.float32)
    for t, tab in enumerate(tabs):
        out[t, :DEC_SEQ] = tab
        out[t, DEC_SEQ:] = 1.0 if t % 3 == 0 else 0.0
    return out


def _dft_mats(n):
    k = np.arange(n)
    kn = (k[:, None] * k[None, :]) % n
    ang = 2.0 * np.pi * kn.astype(np.float64) / n
    return np.cos(ang), -np.sin(ang)


def _mods_kernel(c_ref, w_ref, b_ref, o_ref):
    x = _silu(c_ref[...]).astype(BF16)
    o_ref[...] = _dot(x, w_ref[...].astype(BF16)) + b_ref[...]


def _mods_call(cond, w_ada, b_ada):
    tn = 1024
    nw = N_MODS * D
    return pl.pallas_call(
        _mods_kernel,
        grid=(DEPTH, nw // tn),
        in_specs=[
            pl.BlockSpec((MOD_ROWS, D), lambda l, j: (0, 0)),
            pl.BlockSpec((None, D, tn), lambda l, j: (l, 0, j)),
            pl.BlockSpec((None, 1, tn), lambda l, j: (l, 0, j)),
        ],
        out_specs=pl.BlockSpec((None, MOD_ROWS, tn), lambda l, j: (l, 0, j)),
        out_shape=jax.ShapeDtypeStruct((DEPTH, MOD_ROWS, nw), F32),
        compiler_params=_cparams("arbitrary", "arbitrary"),
        name="ada_mods",
    )(cond, w_ada, b_ada.reshape(DEPTH, 1, nw))


def _mod_spec(l, m, tm, ngrid):
    def idx(*g):
        r = jnp.minimum((g[0] * tm) // DEC_SEQ, CTX_ROW)
        return (l * MOD_ROWS * N_MODS + r * N_MODS + m, 0, 0)
    del ngrid
    return pl.BlockSpec((None, 1, D), idx)


def _ln_spec(l, which):
    return pl.BlockSpec((None, 1, D), lambda *g: (l * 3 + which, 0, 0))


FFN_TM = 512
FFN_TF = 256


def _ffn_kernel(ns_tiles, split_in, split_out, nparts, *refs):
    refs = list(refs)
    x_refs = [refs.pop(0) for _ in range(2 if split_in else 1)]
    if nparts:
        mix_gt_ref = refs.pop(0)
        part_refs = [refs.pop(0) for _ in range(2 * nparts)]
        part_w_refs = [refs.pop(0) for _ in range(nparts)]
        mix_g_ref, mix_b_ref = refs.pop(0), refs.pop(0)
    sh_ref, sc_ref, gt_ref, wg_ref, wu_ref, wd_ref, g_ref, b_ref = [refs.pop(0) for _ in range(8)]
    o_refs = [refs.pop(0) for _ in range(2 if split_out else 1)]
    h_scr, a_scr = refs.pop(0), refs.pop(0)
    latent = pl.program_id(0) < ns_tiles

    def load_x():
        if split_in:
            return jnp.where(latent, x_refs[0][...], x_refs[1][...])
        return x_refs[0][...]

    if nparts:
        y_scr = refs.pop(0)
        mixed = None
        for k in range(nparts):
            part = jnp.where(latent, part_refs[2 * k][...], part_refs[2 * k + 1][...])
            t = _dot(part, part_w_refs[k][...])
            mixed = t if mixed is None else mixed + t
        y_scr[...] = _layer_norm(ALPHA * load_x() + mix_gt_ref[...] * mixed, mix_g_ref[...], mix_b_ref[...])

        def load_x():
            return y_scr[...]

    h_scr[...] = (load_x() * (1.0 + sc_ref[...]) + sh_ref[...]).astype(BF16)
    for c in range(D_FF // FFN_TF):
        cols = slice(c * FFN_TF, (c + 1) * FFN_TF)
        h = h_scr[...]
        a_scr[:, cols] = (_silu(_dot(h, wg_ref[:, cols])) * _dot(h, wu_ref[:, cols])).astype(BF16)
    down = _dot(a_scr[...], wd_ref[...])
    z = ALPHA * load_x() + gt_ref[...] * (FFN_HALF * down)
    out = _layer_norm(z, g_ref[...], b_ref[...])
    if split_out:
        @pl.when(latent)
        def _():
            o_refs[0][...] = out

        @pl.when(jnp.logical_not(latent))
        def _():
            o_refs[1][...] = out
    else:
        o_refs[0][...] = out


def _stream_specs(tm, w, ns_tiles):
    return [pl.BlockSpec((tm, w), lambda i: (jnp.minimum(i, ns_tiles - 1), 0)),
            pl.BlockSpec((tm, w), lambda i: (jnp.maximum(i - ns_tiles, 0), 0))]


def _ffn_call(ys, modsr, lng, lnb, wg, wu, wd, l, k, split_out=False, mixer_parts=()):
    tm = FFN_TM
    ns_tiles = NS // tm
    split_in = isinstance(ys, tuple)
    m0 = 0 if k == 0 else 6
    which = 0 if k == 0 else 2
    joint = pl.BlockSpec((tm, D), lambda i: (i, 0))
    resident = pl.Buffered(1)
    in_specs = _stream_specs(tm, D, ns_tiles) if split_in else [joint]
    args = list(ys) if split_in else [ys]
    scratch = [pltpu.VMEM((tm, D), BF16), pltpu.VMEM((tm, D_FF), BF16)]
    if mixer_parts:
        in_specs.append(_mod_spec(l, 5, tm, 1))
        args.append(modsr)
        for xs, xp, _, _ in mixer_parts:
            in_specs += _stream_specs(tm, xs.shape[1], ns_tiles)
            args += [xs, xp]
        for _, _, warr, wspec in mixer_parts:
            in_specs.append(wspec)
            args.append(warr)
        in_specs += [_ln_spec(l, 1), _ln_spec(l, 1)]
        args += [lng, lnb]
        scratch.append(pltpu.VMEM((tm, D), F32))
    in_specs += [
        _mod_spec(l, m0, tm, 1), _mod_spec(l, m0 + 1, tm, 1), _mod_spec(l, m0 + 2, tm, 1),
        pl.BlockSpec((None, None, D, D_FF), lambda i: (l, k, 0, 0), pipeline_mode=resident),
        pl.BlockSpec((None, None, D, D_FF), lambda i: (l, k, 0, 0), pipeline_mode=resident),
        pl.BlockSpec((None, None, D_FF, D), lambda i: (l, k, 0, 0), pipeline_mode=resident),
        _ln_spec(l, which), _ln_spec(l, which),
    ]
    args += [modsr, modsr, modsr, wg, wu, wd, lng, lnb]
    if split_out:
        out_specs = _stream_specs(tm, D, ns_tiles)
        out_shape = [jax.ShapeDtypeStruct((NS, D), F32), jax.ShapeDtypeStruct((NP, D), F32)]
    else:
        out_specs = joint
        out_shape = jax.ShapeDtypeStruct((NTOK, D), F32)
    return pl.pallas_call(
        functools.partial(_ffn_kernel, ns_tiles, split_in, split_out, len(mixer_parts)),
        grid=(NTOK // tm,),
        in_specs=in_specs,
        out_specs=out_specs,
        out_shape=out_shape,
        scratch_shapes=scratch,
        compiler_params=_cparams("arbitrary"),
        name="half_ffn",
    )(*args)


INPROJ_TM = 1024
INPROJ_SUBTILES = 2


def _prep_ab_weights(ab_w_in, b_g_cq, b_w_uq, b_g_ckv, b_w_ukv):
    w_in = jnp.pad(ab_w_in, ((0, 0), (0, 0), (0, AB_IN_PAD - AB_IN))).astype(BF16)
    w_uq = jnp.pad(b_w_uq.reshape(N_AB, B_Q_LORA, B_HEADS, B_NOPE + B_ROPE),
                   ((0, 0), (0, 0), (0, 0), (0, LANES - B_NOPE - B_ROPE))).reshape(N_AB, B_Q_LORA, 1024).astype(BF16)
    ukv = b_w_ukv.reshape(N_AB, B_KV_LORA, B_HEADS, B_NOPE + B_V)
    w_uk = jnp.pad(ukv[..., :B_NOPE], ((0, 0), (0, 0), (0, 0), (0, LANES - B_NOPE))).reshape(N_AB, B_KV_LORA, 1024)
    place = np.zeros((LANES, B_HEADS, LANES), np.float32)
    for d in range(B_ROPE):
        place[d, :, B_NOPE + d] = 1.0
    w_k = jnp.concatenate(
        [w_uk, jnp.broadcast_to(jnp.asarray(place.reshape(LANES, 1024)), (N_AB, LANES, 1024))], axis=1).astype(BF16)
    w_v = ukv[..., B_NOPE:].reshape(N_AB, B_KV_LORA, 512).astype(BF16)
    g_cq = b_g_cq.reshape(N_AB, 1, B_Q_LORA)
    g_ckv = b_g_ckv.reshape(N_AB, 1, B_KV_LORA)
    return w_in, g_cq, w_uq, g_ckv, w_k, w_v


def _rope(x, cos, sin_m, sin_p, quarter):
    n = x.shape[1] // LANES
    up = pltpu.roll(x, x.shape[1] - quarter, 1)
    dn = pltpu.roll(x, quarter, 1)
    if n > 1:
        cos = jnp.concatenate([cos] * n, axis=1)
        sin_m = jnp.concatenate([sin_m] * n, axis=1)
        sin_p = jnp.concatenate([sin_p] * n, axis=1)
    return x * cos + up * sin_m + dn * sin_p


def _inproj_kernel(ns_tiles, x_ref, sh_ref, sc_ref, w_in_ref, gq_ref, wq_ref, gkv_ref, wk_ref, wv_ref, rope_ref,
                   qa_ref, ka_ref, va_ref, qb_ref, kb_ref, vb_ref,
                   cka_ref, cva_ref, cckv_ref, ckpe_ref):
    i = pl.program_id(0)
    lo = _lane_lo()
    sub = x_ref.shape[0] // INPROJ_SUBTILES
    for t in range(INPROJ_SUBTILES):
        rows = slice(t * sub, (t + 1) * sub)

        def rope(x, tab, quarter):
            return _rope(x, rope_ref[tab, rows, :], rope_ref[tab + 1, rows, :], rope_ref[tab + 2, rows, :], quarter)

        h = (x_ref[rows, :] * (1.0 + sc_ref[...]) + sh_ref[...]).astype(BF16)
        proj = _dot(h, w_in_ref[...])
        qa = proj[:, 0:512]
        ka = proj[:, 512:640]
        va = proj[:, 640:768]
        cq = proj[:, 768:1152]
        ckv = proj[:, 1152:1408]
        kpe = proj[:, 1408:1536]
        ckv_n = _rms_norm(ckv, gkv_ref[...])

        @pl.when(i >= ns_tiles)
        def _():
            cka_ref[rows, :] = ka
            cva_ref[rows, :] = va
            cckv_ref[rows, :] = ckv_n
            ckpe_ref[rows, :] = kpe[:, :B_ROPE]

        qa_ref[rows, :] = (rope(qa, 0, A_HEAD_DIM // 4) * (A_SCALE * LOG2E)).astype(BF16)
        ka_r = rope(ka, 0, A_HEAD_DIM // 4)
        ka_s = pltpu.roll(ka_r, A_HEAD_DIM, 1)
        zero = jnp.zeros_like(ka_r)
        ka_ref[rows, :] = jnp.concatenate(
            [jnp.where(lo, ka_r, zero), jnp.where(lo, zero, ka_s), jnp.where(lo, ka_s, zero),
             jnp.where(lo, zero, ka_r)], axis=1).astype(BF16)
        va_s = pltpu.roll(va, A_HEAD_DIM, 1)
        va_ref[rows, :] = jnp.concatenate([jnp.where(lo, va, va_s), jnp.where(lo, va_s, va)], axis=1).astype(BF16)

        cq_n = _rms_norm(cq, gq_ref[...]).astype(BF16)
        qb = _dot(cq_n, wq_ref[...])
        qb_ref[rows, :] = (rope(qb, 3, B_ROPE // 4) * (B_SCALE * LOG2E)).astype(BF16)
        kpe_r = rope(kpe, 6, B_ROPE // 4)
        lat = jnp.concatenate([ckv_n.astype(BF16), kpe_r.astype(BF16)], axis=1)
        kb_ref[rows, :] = _dot(lat, wk_ref[...]).astype(BF16)
        vb_ref[rows, :] = _dot(lat[:, :B_KV_LORA], wv_ref[...]).astype(BF16)


def _inproj_call(y, modsr, l, w_in, g_cq, w_uq, g_ckv, w_k, w_v, rope_tabs, tm):
    ns_tiles = NS // tm
    pos_tiles = DEC_SEQ // tm

    def tok(w):
        return pl.BlockSpec((tm, w), lambda i: (i, 0))

    def ctx(w):
        return pl.BlockSpec((tm, w), lambda i: (jnp.maximum(i - ns_tiles, 0), 0))

    def full(a):
        return pl.BlockSpec(a.shape, lambda i: (0,) * a.ndim)

    def rope_idx(i):
        return (0, jnp.where(i < ns_tiles, i % pos_tiles, pos_tiles), 0)

    outs = pl.pallas_call(
        functools.partial(_inproj_kernel, ns_tiles),
        grid=(NTOK // tm,),
        in_specs=[
            tok(D), _mod_spec(l, 3, tm, 1), _mod_spec(l, 4, tm, 1),
            full(w_in), full(g_cq), full(w_uq), full(g_ckv), full(w_k), full(w_v),
            pl.BlockSpec((9, tm, LANES), rope_idx),
        ],
        out_specs=[tok(512), tok(512), tok(256), tok(1024), tok(1024), tok(512),
                   ctx(128), ctx(128), ctx(B_KV_LORA), ctx(B_ROPE)],
        out_shape=[
            jax.ShapeDtypeStruct((NTOK, 512), BF16), jax.ShapeDtypeStruct((NTOK, 512), BF16),
            jax.ShapeDtypeStruct((NTOK, 256), BF16), jax.ShapeDtypeStruct((NTOK, 1024), BF16),
            jax.ShapeDtypeStruct((NTOK, 1024), BF16), jax.ShapeDtypeStruct((NTOK, 512), BF16),
            jax.ShapeDtypeStruct((NP, 128), F32), jax.ShapeDtypeStruct((NP, 128), F32),
            jax.ShapeDtypeStruct((NP, B_KV_LORA), F32), jax.ShapeDtypeStruct((NP, B_ROPE), F32),
        ],
        compiler_params=_cparams("arbitrary"),
        name="ab_inproj",
    )(y, modsr, modsr, w_in, g_cq, w_uq, g_ckv, w_k, w_v, rope_tabs)
    return outs


def _ctxkv_kernel(ckv_ref, kpe_ref, wk_ref, wv_ref, k_ref, v_ref):
    ckv = ckv_ref[...].astype(BF16)
    k = _dot(ckv, wk_ref[:B_KV_LORA, :]) + _dot(kpe_ref[...].astype(BF16), wk_ref[B_KV_LORA:B_KV_LORA + B_ROPE, :])
    k_ref[...] = k.astype(BF16)
    v_ref[...] = _dot(ckv, wv_ref[...]).astype(BF16)


def _ctxkv_call(cache_ckv, cache_kpe, li, w_k, w_v):
    return pl.pallas_call(
        _ctxkv_kernel,
        grid=(DEC_BATCH,),
        in_specs=[
            pl.BlockSpec((None, None, PAST, B_KV_LORA), lambda b: (b, li, 0, 0)),
            pl.BlockSpec((None, None, PAST, B_ROPE), lambda b: (b, li, 0, 0)),
            pl.BlockSpec(w_k.shape, lambda b: (0, 0)),
            pl.BlockSpec(w_v.shape, lambda b: (0, 0)),
        ],
        out_specs=[pl.BlockSpec((PAST, 1024), lambda b: (b, 0)), pl.BlockSpec((PAST, 512), lambda b: (b, 0))],
        out_shape=[jax.ShapeDtypeStruct((DEC_BATCH * PAST, 1024), BF16),
                   jax.ShapeDtypeStruct((DEC_BATCH * PAST, 512), BF16)],
        compiler_params=_cparams("arbitrary"),
        name="b_ctx_kv",
    )(cache_ckv, cache_kpe, w_k, w_v)


def _softmax_pv(s, values, sink):
    m = jnp.max(s, axis=-1, keepdims=True)
    if sink is not None:
        m = jnp.maximum(m, sink)
    p = jnp.exp2(s - m)
    den = jnp.sum(p, axis=-1, keepdims=True)
    if sink is not None:
        den = den + jnp.exp2(sink - m)
    p = p.astype(BF16)
    out, c0 = None, 0
    for v in values:
        o = _dot(p[:, c0:c0 + v.shape[0]], v)
        c0 += v.shape[0]
        out = o if out is None else out + o
    return out * (1.0 / den)


def _lane_lo():
    return lax.broadcasted_iota(jnp.int32, (1, LANES), 1) < A_HEAD_DIM


A_LAT_BLOCKS = 2


def _attn_a_lat_kernel(sink_ref, q_ref, k_ref, v_ref, kc_ref, vc_ref, o_ref):
    nb = DEC_SEQ // BAND
    lo = _lane_lo()
    kc = kc_ref[...]
    vc = vc_ref[...]
    kc_s = pltpu.roll(kc, A_HEAD_DIM, 1)
    vc_s = pltpu.roll(vc, A_HEAD_DIM, 1)
    zero = jnp.zeros_like(kc)
    kc_var = [jnp.where(lo, kc, zero), jnp.where(lo, zero, kc_s), jnp.where(lo, kc_s, zero), jnp.where(lo, zero, kc)]
    kc_var = [k.astype(BF16) for k in kc_var]
    vc_dup = [jnp.where(lo, vc, vc_s).astype(BF16), jnp.where(lo, vc_s, vc).astype(BF16)]

    assert WINDOW == BAND
    qi = lax.broadcasted_iota(jnp.int32, (2 * BAND, BAND), 0) % BAND
    kj = lax.broadcasted_iota(jnp.int32, (2 * BAND, BAND), 1)
    top = lax.broadcasted_iota(jnp.int32, (2 * BAND, 1), 0) < BAND

    for sub in range(A_LAT_BLOCKS):
        n = pl.program_id(1) * A_LAT_BLOCKS + sub
        rows = [pl.multiple_of(jnp.maximum(n - 1, 0) * BAND, BAND), pl.multiple_of(n * BAND, BAND),
                pl.multiple_of(jnp.minimum(n + 1, nb - 1) * BAND, BAND)]
        t_min = jnp.where(n > 0, 0, BAND)
        t_max = jnp.where(n < nb - 1, 3 * BAND - 1, 2 * BAND - 1)
        ok_prev = kj >= jnp.maximum(qi, t_min)
        ok_next = kj + 2 * BAND <= jnp.minimum(qi + 2 * BAND, t_max)
        q0 = sub * BAND
        for hk in range(A_KV_HEADS):
            q2 = jnp.concatenate([q_ref[q0:q0 + BAND, (2 * hk) * LANES:(2 * hk + 1) * LANES],
                                  q_ref[q0:q0 + BAND, (2 * hk + 1) * LANES:(2 * hk + 2) * LANES]], axis=0)
            vs = jnp.concatenate(
                [vc_dup[hk]] + [v_ref[pl.ds(r, BAND), hk * LANES:(hk + 1) * LANES] for r in rows], axis=0)
            res = []
            for e in range(2):
                c0 = (2 * hk + e) * LANES
                ks = jnp.concatenate([kc_var[2 * hk + e]] + [k_ref[pl.ds(r, BAND), c0:c0 + LANES] for r in rows], axis=0)
                s = _dot_nt(q2, ks)
                s = jnp.concatenate([s[:, :PAST], jnp.where(ok_prev, s[:, PAST:PAST + BAND], NEG_INF),
                                     s[:, PAST + BAND:PAST + 2 * BAND],
                                     jnp.where(ok_next, s[:, PAST + 2 * BAND:], NEG_INF)], axis=1)
                sink = jnp.where(top, sink_ref[4 * hk + e], sink_ref[4 * hk + 2 + e]) * LOG2E
                res.append(_softmax_pv(s, [vs], sink))
            o2 = jnp.where(lo, res[0], res[1]).astype(BF16)
            o_ref[q0:q0 + BAND, (2 * hk) * LANES:(2 * hk + 1) * LANES] = o2[:BAND]
            o_ref[q0:q0 + BAND, (2 * hk + 1) * LANES:(2 * hk + 2) * LANES] = o2[BAND:]


def _attn_a_lat_call(sink, qa, ka, va, cache_k, cache_v, li):
    tq = A_LAT_BLOCKS * BAND
    nb = DEC_SEQ // tq
    return pl.pallas_call(
        _attn_a_lat_kernel,
        grid=(DEC_BATCH, nb),
        in_specs=[
            pl.BlockSpec(memory_space=pltpu.SMEM),
            pl.BlockSpec((tq, 512), lambda b, n: (b * nb + n, 0)),
            pl.BlockSpec((DEC_SEQ, 512), lambda b, n: (b, 0)),
            pl.BlockSpec((DEC_SEQ, 256), lambda b, n: (b, 0)),
            pl.BlockSpec((None, None, PAST, 128), lambda b, n: (b, li, 0, 0)),
            pl.BlockSpec((None, None, PAST, 128), lambda b, n: (b, li, 0, 0)),
        ],
        out_specs=pl.BlockSpec((tq, 512), lambda b, n: (b * nb + n, 0)),
        out_shape=jax.ShapeDtypeStruct((NS, 512), BF16),
        compiler_params=_cparams("arbitrary", "arbitrary"),
        name="attn_a_latent",
    )(sink, qa, ka, va, cache_k, cache_v)


def _attn_b_lat_kernel(q_ref, k_ref, v_ref, kc_ref, vc_ref, o_ref):
    lo = _lane_lo()
    for j in range(B_HEADS // 2):
        vs = [vc_ref[:, j * LANES:(j + 1) * LANES], v_ref[:, j * LANES:(j + 1) * LANES]]
        res = []
        for e in range(2):
            c0 = (2 * j + e) * LANES
            q = q_ref[:, c0:c0 + LANES]
            s = jnp.concatenate([_dot_nt(q, kc_ref[:, c0:c0 + LANES]), _dot_nt(q, k_ref[:, c0:c0 + LANES])], axis=1)
            res.append(_softmax_pv(s, vs, None))
        o_ref[:, j * LANES:(j + 1) * LANES] = jnp.where(lo, res[0], res[1]).astype(BF16)


def _attn_b_lat_call(qb, kb, vb, kc, vc):
    tq = 256
    nq = DEC_SEQ // tq
    return pl.pallas_call(
        _attn_b_lat_kernel,
        grid=(DEC_BATCH, nq),
        in_specs=[
            pl.BlockSpec((tq, 1024), lambda b, n: (b * nq + n, 0)),
            pl.BlockSpec((DEC_SEQ, 1024), lambda b, n: (b, 0)),
            pl.BlockSpec((DEC_SEQ, 512), lambda b, n: (b, 0)),
            pl.BlockSpec((PAST, 1024), lambda b, n: (b, 0)),
            pl.BlockSpec((PAST, 512), lambda b, n: (b, 0)),
        ],
        out_specs=pl.BlockSpec((tq, 512), lambda b, n: (b * nq + n, 0)),
        out_shape=jax.ShapeDtypeStruct((NS, 512), BF16),
        compiler_params=_cparams("arbitrary", "arbitrary"),
        name="attn_b_latent",
    )(qb, kb, vb, kc, vc)


def _attn_ctx_kernel(sink_ref, qa_ref, ka_ref, va_ref, qb_ref, kb_ref, vb_ref, oa_ref, ob_ref):
    lo = _lane_lo()
    top = lax.broadcasted_iota(jnp.int32, (2 * SEQ, 1), 0) < SEQ
    for hk in range(A_KV_HEADS):
        q2 = jnp.concatenate([qa_ref[:, (2 * hk) * LANES:(2 * hk + 1) * LANES],
                              qa_ref[:, (2 * hk + 1) * LANES:(2 * hk + 2) * LANES]], axis=0)
        v = va_ref[:, hk * LANES:(hk + 1) * LANES]
        res = []
        for e in range(2):
            c0 = (2 * hk + e) * LANES
            s = _dot_nt(q2, ka_ref[:, c0:c0 + LANES])
            sink = jnp.where(top, sink_ref[4 * hk + e], sink_ref[4 * hk + 2 + e]) * LOG2E
            res.append(_softmax_pv(s, [v], sink))
        o2 = jnp.where(lo, res[0], res[1]).astype(BF16)
        oa_ref[:, (2 * hk) * LANES:(2 * hk + 1) * LANES] = o2[:SEQ]
        oa_ref[:, (2 * hk + 1) * LANES:(2 * hk + 2) * LANES] = o2[SEQ:]
    for j in range(B_HEADS // 2):
        v = vb_ref[:, j * LANES:(j + 1) * LANES]
        res = []
        for e in range(2):
            c0 = (2 * j + e) * LANES
            s = _dot_nt(qb_ref[:, c0:c0 + LANES], kb_ref[:, c0:c0 + LANES])
            res.append(_softmax_pv(s, [v], None))
        ob_ref[:, j * LANES:(j + 1) * LANES] = jnp.where(lo, res[0], res[1]).astype(BF16)


def _attn_ctx_call(sink, qa, ka, va, qb, kb, vb):
    off = NS // SEQ

    def tok(w):
        return pl.BlockSpec((SEQ, w), lambda b: (off + b, 0))

    def out(w):
        return pl.BlockSpec((SEQ, w), lambda b: (b, 0))

    return pl.pallas_call(
        _attn_ctx_kernel,
        grid=(BATCH,),
        in_specs=[pl.BlockSpec(memory_space=pltpu.SMEM), tok(512), tok(512), tok(256), tok(1024), tok(1024), tok(512)],
        out_specs=[out(512), out(512)],
        out_shape=[jax.ShapeDtypeStruct((NP, 512), BF16), jax.ShapeDtypeStruct((NP, 512), BF16)],
        compiler_params=_cparams("arbitrary"),
        name="attn_context",
    )(sink, qa, ka, va, qb, kb, vb)


def _fourier_in_kernel(x_ref, sh_ref, sc_ref, w_ref, cs_ref, p_ref):
    h = (x_ref[...] * (1.0 + sc_ref[...]) + sh_ref[...]).astype(BF16)
    u = _dot(h, w_ref[...]).astype(BF16)
    for g in range(F_GROUPS):
        pcs = _dot(u[:, g * F_GROUP_DIM:(g + 1) * F_GROUP_DIM], cs_ref[...]).astype(BF16)
        p_ref[:, g * F_GROUP_DIM:(g + 1) * F_GROUP_DIM] = pcs[:, :F_GROUP_DIM]
        p_ref[:, D + g * F_GROUP_DIM:D + (g + 1) * F_GROUP_DIM] = pcs[:, F_GROUP_DIM:]


def _fourier_in_call(y, modsr, l, w_in, cs, tm=512):
    return pl.pallas_call(
        _fourier_in_kernel,
        grid=(NTOK // tm,),
        in_specs=[
            pl.BlockSpec((tm, D), lambda i: (i, 0)), _mod_spec(l, 3, tm, 1), _mod_spec(l, 4, tm, 1),
            pl.BlockSpec(w_in.shape, lambda i: (0, 0)), pl.BlockSpec(cs.shape, lambda i: (0, 0)),
        ],
        out_specs=pl.BlockSpec((tm, 2 * D), lambda i: (i, 0)),
        out_shape=jax.ShapeDtypeStruct((NTOK, 2 * D), BF16),
        compiler_params=_cparams("arbitrary"),
        name="fourier_in",
    )(y, modsr, modsr, w_in, cs)


def _fourier_pos_kernel(norm, c_ref, s_ref, p_ref, f_ref):
    f = _dot(c_ref[...], p_ref[:, :D]) + _dot(s_ref[...], p_ref[:, D:])
    f_ref[...] = (f * norm).astype(BF16)


def _fourier_pos_call(p, cmat, smat, length, nbatch, row0):
    tk = min(length, 256)
    nk = length // tk
    off = row0 // length
    norm = float((length * F_GROUP_DIM) ** -0.5)
    return pl.pallas_call(
        functools.partial(_fourier_pos_kernel, norm),
        grid=(nbatch, nk),
        in_specs=[
            pl.BlockSpec((tk, length), lambda b, k: (k, 0)),
            pl.BlockSpec((tk, length), lambda b, k: (k, 0)),
            pl.BlockSpec((length, 2 * D), lambda b, k: (off + b, 0)),
        ],
        out_specs=pl.BlockSpec((tk, D), lambda b, k: (b * nk + k, 0)),
        out_shape=jax.ShapeDtypeStruct((nbatch * length, D), BF16),
        compiler_params=_cparams("arbitrary", "arbitrary"),
        name="fourier_pos",
    )(cmat, smat, p)


CT_N1 = 4
CT_N2 = DEC_SEQ // CT_N1
CT_ROWS = 64
CT_COLS = 256


def _ct_tables():
    n2 = np.arange(CT_N2)
    tw = np.zeros((2, CT_N1 - 1, CT_N2, LANES), np.float32)
    for k1 in range(1, CT_N1):
        ang = 2.0 * np.pi * ((n2 * k1) % DEC_SEQ).astype(np.float64) / DEC_SEQ
        tw[0, k1 - 1] = np.cos(ang)[:, None]
        tw[1, k1 - 1] = np.sin(ang)[:, None]
    c2, s2 = _dft_mats(CT_N2)
    return tw, np.concatenate([c2, -s2], axis=1)


def _fourier_pos_ct_kernel(norm, p_ref, tw_ref, cs_ref, f_ref, b_scr):
    n2 = CT_N2
    for q in range(D // CT_COLS):
        re = slice(q * CT_COLS, (q + 1) * CT_COLS)
        im = slice(D + q * CT_COLS, D + (q + 1) * CT_COLS)
        for r0 in range(0, n2, CT_ROWS):
            a = [p_ref[j * n2 + r0:j * n2 + r0 + CT_ROWS, re].astype(F32) for j in range(CT_N1)]
            b = [p_ref[j * n2 + r0:j * n2 + r0 + CT_ROWS, im].astype(F32) for j in range(CT_N1)]
            s02a, d02a, s13a, d13a = a[0] + a[2], a[0] - a[2], a[1] + a[3], a[1] - a[3]
            s02b, d02b, s13b, d13b = b[0] + b[2], b[0] - b[2], b[1] + b[3], b[1] - b[3]
            br = [s02a + s13a, d02a + d13b, s02a - s13a, d02a - d13b]
            bi = [s02b + s13b, d02b - d13a, s02b - s13b, d02b + d13a]
            for k1 in range(CT_N1):
                r, i = br[k1], bi[k1]
                if k1 > 0:
                    tc = jnp.concatenate([tw_ref[0, k1 - 1, r0:r0 + CT_ROWS, :]] * (CT_COLS // LANES), axis=1)
                    ts = jnp.concatenate([tw_ref[1, k1 - 1, r0:r0 + CT_ROWS, :]] * (CT_COLS // LANES), axis=1)
                    r, i = r * tc + i * ts, i * tc - r * ts
                b_scr[k1, r0:r0 + CT_ROWS, re] = r.astype(BF16)
                b_scr[k1, n2 + r0:n2 + r0 + CT_ROWS, re] = i.astype(BF16)
        for k1 in range(CT_N1):
            f = _dot(cs_ref[...], b_scr[k1, :, re]) * norm
            f_ref[:, k1 * D + q * CT_COLS:k1 * D + (q + 1) * CT_COLS] = f.astype(BF16)


def _fourier_pos_ct_call(p, tw, cs):
    norm = float((DEC_SEQ * F_GROUP_DIM) ** -0.5)
    out = pl.pallas_call(
        functools.partial(_fourier_pos_ct_kernel, norm),
        grid=(DEC_BATCH,),
        in_specs=[
            pl.BlockSpec((DEC_SEQ, 2 * D), lambda b: (b, 0)),
            pl.BlockSpec(tw.shape, lambda b: (0, 0, 0, 0)),
            pl.BlockSpec(cs.shape, lambda b: (0, 0)),
        ],
        out_specs=pl.BlockSpec((CT_N2, CT_N1 * D), lambda b: (b, 0)),
        out_shape=jax.ShapeDtypeStruct((DEC_BATCH * CT_N2, CT_N1 * D), BF16),
        scratch_shapes=[pltpu.VMEM((CT_N1, 2 * CT_N2, D), BF16)],
        compiler_params=_cparams("arbitrary"),
        name="fourier_pos_ct",
    )(p, tw, cs)
    return out.reshape(NS, D)


def kernel(x_prompt, x_sample, cache_a_k, cache_a_v, cache_b_ckv, cache_b_kpe, c, c_ctx, w_ada, b_ada, ln_g, ln_b,
           ffn_w_gate, ffn_w_up, ffn_w_down, ab_w_in, ab_w_out, a_sink, b_g_cq, b_w_uq, b_g_ckv, b_w_ukv,
           f_w_in, f_w_out):
    tm_in = INPROJ_TM
    y = (x_sample.reshape(NS, D), x_prompt.reshape(NP, D))
    cond = jnp.concatenate([c, c_ctx[None, :], jnp.zeros((MOD_ROWS - DEC_BATCH - 1, D), F32)], axis=0)
    lng = ln_g.reshape(DEPTH * 3, 1, D)
    lnb = ln_b.reshape(DEPTH * 3, 1, D)
    wg = ffn_w_gate.astype(BF16)
    wu = ffn_w_up.astype(BF16)
    wd = ffn_w_down.astype(BF16)
    w_out = ab_w_out.astype(BF16)
    w_in, g_cq, w_uq, g_ckv, w_k, w_v = _prep_ab_weights(ab_w_in, b_g_cq, b_w_uq, b_g_ckv, b_w_ukv)
    fw_in = f_w_in.astype(BF16)
    fw_out = f_w_out.astype(BF16)
    cache_k = cache_a_k.reshape(DEC_BATCH, N_AB, PAST, A_KV_HEADS * A_HEAD_DIM)
    cache_v = cache_a_v.reshape(DEC_BATCH, N_AB, PAST, A_KV_HEADS * A_HEAD_DIM)
    rope_tabs = jnp.asarray(_rope_tables(tm_in))
    cc, sc = _dft_mats(F_GROUP_DIM)
    cs_chan = jnp.asarray(np.concatenate([cc, sc], axis=1), dtype=BF16)
    cp, sp = _dft_mats(SEQ)
    dft_ctx = (jnp.asarray(cp, dtype=BF16), jnp.asarray(-sp, dtype=BF16))
    ct_tw, ct_cs = _ct_tables()
    ct_tw = jnp.asarray(ct_tw)
    ct_cs = jnp.asarray(ct_cs, dtype=BF16)

    mods = _mods_call(cond, w_ada, b_ada)
    modsr = mods.reshape(DEPTH * MOD_ROWS * N_MODS, 1, D)

    new_k, new_v, new_ckv, new_kpe = [], [], [], []
    for l in range(DEPTH):
        li = l // 2
        y = _ffn_call(y, modsr, lng, lnb, wg, wu, wd, l, 0)
        if l % 2 == 0:
            (qa, ka, va, qb, kb, vb, cka, cva, cckv, ckpe) = _inproj_call(
                y, modsr, l, w_in[li], g_cq[li], w_uq[li], g_ckv[li], w_k[li], w_v[li], rope_tabs, tm_in)
            kc, vc = _ctxkv_call(cache_b_ckv, cache_b_kpe, li, w_k[li], w_v[li])
            oa_s = _attn_a_lat_call(a_sink[li], qa, ka, va, cache_k, cache_v, li)
            ob_s = _attn_b_lat_call(qb, kb, vb, kc, vc)
            oa_p, ob_p = _attn_ctx_call(a_sink[li], qa, ka, va, qb, kb, vb)
            parts = [
                (oa_s, oa_p, w_out, pl.BlockSpec((None, 512, D), lambda i, li=li: (li, 0, 0))),
                (ob_s, ob_p, w_out, pl.BlockSpec((None, 512, D), lambda i, li=li: (li, 1, 0))),
            ]
            new_k.append(cka.reshape(BATCH, SEQ, A_KV_HEADS, A_HEAD_DIM))
            new_v.append(cva.reshape(BATCH, SEQ, A_KV_HEADS, A_HEAD_DIM))
            new_ckv.append(cckv.reshape(BATCH, SEQ, B_KV_LORA))
            new_kpe.append(ckpe.reshape(BATCH, SEQ, B_ROPE))
        else:
            p = _fourier_in_call(y, modsr, l, fw_in[li], cs_chan)
            f_s = _fourier_pos_ct_call(p, ct_tw, ct_cs)
            f_p = _fourier_pos_call(p, dft_ctx[0], dft_ctx[1], SEQ, BATCH, NS)
            parts = [(f_s, f_p, fw_out, pl.BlockSpec((None, D, D), lambda i, li=li: (li, 0, 0)))]
        y = _ffn_call(y, modsr, lng, lnb, wg, wu, wd, l, 1, split_out=(l == DEPTH - 1), mixer_parts=parts)

    y_sample = y[0].reshape(DEC_BATCH, DEC_SEQ, D)
    y_prompt = y[1].reshape(BATCH, SEQ, D)
    return (y_prompt, y_sample, jnp.stack(new_k, axis=1), jnp.stack(new_v, axis=1),
            jnp.stack(new_ckv, axis=1), jnp.stack(new_kpe, axis=1))
```

```python
import functools

import numpy as np
import jax
import jax.numpy as jnp
from jax import lax
from jax.experimental import pallas as pl
from jax.experimental.pallas import tpu as pltpu

F32 = jnp.float32
BF16 = jnp.bfloat16

D = 1024
BATCH = 16
SEQ = 256
DEPTH = 4
DEC_BATCH = 8
DEC_SEQ = 2048
PAST = 256
GRID_W = 64
N_AB = (DEPTH + 1) // 2
A_HEADS = 8
A_KV_HEADS = 2
A_HEAD_DIM = 64
WINDOW = 128
BAND = 128
B_HEADS = 8
B_NOPE = 64
B_ROPE = 32
B_V = 64
B_Q_LORA = 384
B_KV_LORA = 256
AB_IN = 1440
AB_IN_PAD = 1536
A_SCALE = A_HEAD_DIM ** -0.5
B_SCALE = (B_NOPE + B_ROPE) ** -0.5
F_GROUPS = 4
F_GROUP_DIM = D // F_GROUPS
D_FF = 2816
FFN_HALF = 0.5
ALPHA = (2 * DEPTH) ** 0.25
LN_EPS = 1e-5
RMS_EPS = 1e-6
N_MODS = 9
ROPE_BASE = 10000.0
NEG_INF = -1e30
LOG2E = 1.4426950408889634

NS = DEC_BATCH * DEC_SEQ
NP = BATCH * SEQ
NTOK = NS + NP
MOD_ROWS = 16
CTX_ROW = DEC_BATCH
LANES = 128
VMEM_LIMIT = 56 * 1024 * 1024


def _cparams(*sem):
    return pltpu.CompilerParams(dimension_semantics=sem, vmem_limit_bytes=VMEM_LIMIT)


def _silu(x):
    return x * (1.0 / (1.0 + jnp.exp(-x)))


def _layer_norm(z, g, b):
    mu = jnp.mean(z, axis=-1, keepdims=True)
    zc = z - mu
    var = jnp.mean(zc * zc, axis=-1, keepdims=True)
    return zc * lax.rsqrt(var + LN_EPS) * g + b


def _rms_norm(x, g):
    return x * lax.rsqrt(jnp.mean(x * x, axis=-1, keepdims=True) + RMS_EPS) * g


def _dot(a, b):
    return jnp.dot(a, b, preferred_element_type=F32)


def _dot_nt(a, b):
    return lax.dot_general(a, b, (((1,), (1,)), ((), ())), preferred_element_type=F32)


def _rope_tables(tm):
    pos = np.arange(DEC_SEQ)
    row = (pos // GRID_W).astype(np.float64)
    col = (pos % GRID_W).astype(np.float64)

    def layout(lane_d, quarter):
        cos = np.ones((DEC_SEQ, LANES))
        sin_m = np.zeros((DEC_SEQ, LANES))
        sin_p = np.zeros((DEC_SEQ, LANES))
        for lane in range(LANES):
            d = lane_d[lane]
            if d < 0:
                continue
            q, j = d // quarter, d % quarter
            inv = ROPE_BASE ** (-float(j) / quarter)
            ang = (row if q < 2 else col) * inv
            cos[:, lane] = np.cos(ang)
            if q % 2 == 0:
                sin_m[:, lane] = -np.sin(ang)
            else:
                sin_p[:, lane] = np.sin(ang)
        return [cos, sin_m, sin_p]

    lane_a = [lane % A_HEAD_DIM for lane in range(LANES)]
    lane_b = [lane - B_NOPE if B_NOPE <= lane < B_NOPE + B_ROPE else -1 for lane in range(LANES)]
    lane_k = [lane if lane < B_ROPE else -1 for lane in range(LANES)]
    tabs = layout(lane_a, A_HEAD_DIM // 4) + layout(lane_b, B_ROPE // 4) + layout(lane_k, B_ROPE // 4)
    out = np.zeros((9, DEC_SEQ + tm, LANES), np.float32)
    for t, tab in enumerate(tabs):
        out[t, :DEC_SEQ] = tab
        out[t, DEC_SEQ:] = 1.0 if t % 3 == 0 else 0.0
    return out


def _dft_mats(n):
    k = np.arange(n)
    kn = (k[:, None] * k[None, :]) % n
    ang = 2.0 * np.pi * kn.astype(np.float64) / n
    return np.cos(ang), -np.sin(ang)


def _mods_kernel(c_ref, w_ref, b_ref, o_ref):
    x = _silu(c_ref[...]).astype(BF16)
    o_ref[...] = _dot(x, w_ref[...].astype(BF16)) + b_ref[...]


def _mods_call(cond, w_ada, b_ada):
    tn = 1024
    nw = N_MODS * D
    return pl.pallas_call(
        _mods_kernel,
        grid=(DEPTH, nw // tn),
        in_specs=[
            pl.BlockSpec((MOD_ROWS, D), lambda l, j: (0, 0)),
            pl.BlockSpec((None, D, tn), lambda l, j: (l, 0, j)),
            pl.BlockSpec((None, 1, tn), lambda l, j: (l, 0, j)),
        ],
        out_specs=pl.BlockSpec((None, MOD_ROWS, tn), lambda l, j: (l, 0, j)),
        out_shape=jax.ShapeDtypeStruct((DEPTH, MOD_ROWS, nw), F32),
        compiler_params=_cparams("arbitrary", "arbitrary"),
        name="ada_mods",
    )(cond, w_ada, b_ada.reshape(DEPTH, 1, nw))


def _mod_spec(l, m, tm, ngrid):
    def idx(*g):
        r = jnp.minimum((g[0] * tm) // DEC_SEQ, CTX_ROW)
        return (l * MOD_ROWS * N_MODS + r * N_MODS + m, 0, 0)
    del ngrid
    return pl.BlockSpec((None, 1, D), idx)


def _ln_spec(l, which):
    return pl.BlockSpec((None, 1, D), lambda *g: (l * 3 + which, 0, 0))


FFN_TM = 512
FFN_TF = 256
FFN_SUBTILES = 1


def _ffn_kernel(ns_tiles, split_in, split_out, nparts, *refs):
    refs = list(refs)
    x_refs = [refs.pop(0) for _ in range(2 if split_in else 1)]
    if nparts:
        mix_gt_ref = refs.pop(0)
        part_refs = [refs.pop(0) for _ in range(2 * nparts)]
        part_w_refs = [refs.pop(0) for _ in range(nparts)]
        mix_g_ref, mix_b_ref = refs.pop(0), refs.pop(0)
    sh_ref, sc_ref, gt_ref, wg_ref, wu_ref, wd_ref, g_ref, b_ref = [refs.pop(0) for _ in range(8)]
    o_refs = [refs.pop(0) for _ in range(2 if split_out else 1)]
    h_scr, a_scr = refs.pop(0), refs.pop(0)
    latent = pl.program_id(0) < ns_tiles

    y_scr = refs.pop(0) if nparts else None
    sub = h_scr.shape[0] // FFN_SUBTILES
    for t in range(FFN_SUBTILES):
        rows = slice(t * sub, (t + 1) * sub)

        def load_x():
            if split_in:
                return jnp.where(latent, x_refs[0][rows, :], x_refs[1][rows, :])
            return x_refs[0][rows, :]

        if nparts:
            mixed = None
            for k in range(nparts):
                part = jnp.where(latent, part_refs[2 * k][rows, :], part_refs[2 * k + 1][rows, :])
                p = _dot(part, part_w_refs[k][...])
                mixed = p if mixed is None else mixed + p
            y_scr[rows, :] = _layer_norm(ALPHA * load_x() + mix_gt_ref[...] * mixed, mix_g_ref[...], mix_b_ref[...])

            def load_x():
                return y_scr[rows, :]

        h_scr[rows, :] = (load_x() * (1.0 + sc_ref[...]) + sh_ref[...]).astype(BF16)
        for c in range(D_FF // FFN_TF):
            cols = slice(c * FFN_TF, (c + 1) * FFN_TF)
            h = h_scr[rows, :]
            a_scr[rows, cols] = (_silu(_dot(h, wg_ref[:, cols])) * _dot(h, wu_ref[:, cols])).astype(BF16)
        down = _dot(a_scr[rows, :], wd_ref[...])
        z = ALPHA * load_x() + gt_ref[...] * (FFN_HALF * down)
        out = _layer_norm(z, g_ref[...], b_ref[...])
        if split_out:
            @pl.when(latent)
            def _():
                o_refs[0][rows, :] = out

            @pl.when(jnp.logical_not(latent))
            def _():
                o_refs[1][rows, :] = out
        else:
            o_refs[0][rows, :] = out


def _stream_specs(tm, w, ns_tiles):
    return [pl.BlockSpec((tm, w), lambda i: (jnp.minimum(i, ns_tiles - 1), 0)),
            pl.BlockSpec((tm, w), lambda i: (jnp.maximum(i - ns_tiles, 0), 0))]


def _ffn_call(ys, modsr, lng, lnb, wg, wu, wd, l, k, split_out=False, mixer_parts=()):
    tm = FFN_TM
    ns_tiles = NS // tm
    split_in = isinstance(ys, tuple)
    m0 = 0 if k == 0 else 6
    which = 0 if k == 0 else 2
    joint = pl.BlockSpec((tm, D), lambda i: (i, 0))
    resident = pl.Buffered(1)
    in_specs = _stream_specs(tm, D, ns_tiles) if split_in else [joint]
    args = list(ys) if split_in else [ys]
    scratch = [pltpu.VMEM((tm, D), BF16), pltpu.VMEM((tm, D_FF), BF16)]
    if mixer_parts:
        in_specs.append(_mod_spec(l, 5, tm, 1))
        args.append(modsr)
        for xs, xp, _, _ in mixer_parts:
            in_specs += _stream_specs(tm, xs.shape[1], ns_tiles)
            args += [xs, xp]
        for _, _, warr, wspec in mixer_parts:
            in_specs.append(wspec)
            args.append(warr)
        in_specs += [_ln_spec(l, 1), _ln_spec(l, 1)]
        args += [lng, lnb]
        scratch.append(pltpu.VMEM((tm, D), F32))
    in_specs += [
        _mod_spec(l, m0, tm, 1), _mod_spec(l, m0 + 1, tm, 1), _mod_spec(l, m0 + 2, tm, 1),
        pl.BlockSpec((None, None, D, D_FF), lambda i: (l, k, 0, 0), pipeline_mode=resident),
        pl.BlockSpec((None, None, D, D_FF), lambda i: (l, k, 0, 0), pipeline_mode=resident),
        pl.BlockSpec((None, None, D_FF, D), lambda i: (l, k, 0, 0), pipeline_mode=resident),
        _ln_spec(l, which), _ln_spec(l, which),
    ]
    args += [modsr, modsr, modsr, wg, wu, wd, lng, lnb]
    if split_out:
        out_specs = _stream_specs(tm, D, ns_tiles)
        out_shape = [jax.ShapeDtypeStruct((NS, D), F32), jax.ShapeDtypeStruct((NP, D), F32)]
    else:
        out_specs = joint
        out_shape = jax.ShapeDtypeStruct((NTOK, D), F32)
    return pl.pallas_call(
        functools.partial(_ffn_kernel, ns_tiles, split_in, split_out, len(mixer_parts)),
        grid=(NTOK // tm,),
        in_specs=in_specs,
        out_specs=out_specs,
        out_shape=out_shape,
        scratch_shapes=scratch,
        compiler_params=_cparams("arbitrary"),
        name="half_ffn",
    )(*args)


INPROJ_TM = 1024
INPROJ_SUBTILES = 2


def _prep_ab_weights(ab_w_in, b_g_cq, b_w_uq, b_g_ckv, b_w_ukv):
    w_in = jnp.pad(ab_w_in, ((0, 0), (0, 0), (0, AB_IN_PAD - AB_IN))).astype(BF16)
    w_uq = jnp.pad(b_w_uq.reshape(N_AB, B_Q_LORA, B_HEADS, B_NOPE + B_ROPE),
                   ((0, 0), (0, 0), (0, 0), (0, LANES - B_NOPE - B_ROPE))).reshape(N_AB, B_Q_LORA, 1024).astype(BF16)
    ukv = b_w_ukv.reshape(N_AB, B_KV_LORA, B_HEADS, B_NOPE + B_V)
    w_uk = jnp.pad(ukv[..., :B_NOPE], ((0, 0), (0, 0), (0, 0), (0, LANES - B_NOPE))).reshape(N_AB, B_KV_LORA, 1024)
    place = np.zeros((LANES, B_HEADS, LANES), np.float32)
    for d in range(B_ROPE):
        place[d, :, B_NOPE + d] = 1.0
    w_k = jnp.concatenate(
        [w_uk, jnp.broadcast_to(jnp.asarray(place.reshape(LANES, 1024)), (N_AB, LANES, 1024))], axis=1).astype(BF16)
    w_v = ukv[..., B_NOPE:].reshape(N_AB, B_KV_LORA, 512).astype(BF16)
    g_cq = b_g_cq.reshape(N_AB, 1, B_Q_LORA)
    g_ckv = b_g_ckv.reshape(N_AB, 1, B_KV_LORA)
    return w_in, g_cq, w_uq, g_ckv, w_k, w_v


def _rope(x, cos, sin_m, sin_p, quarter):
    n = x.shape[1] // LANES
    up = pltpu.roll(x, x.shape[1] - quarter, 1)
    dn = pltpu.roll(x, quarter, 1)
    if n > 1:
        cos = jnp.concatenate([cos] * n, axis=1)
        sin_m = jnp.concatenate([sin_m] * n, axis=1)
        sin_p = jnp.concatenate([sin_p] * n, axis=1)
    return x * cos + up * sin_m + dn * sin_p


def _inproj_kernel(ns_tiles, x_ref, sh_ref, sc_ref, w_in_ref, gq_ref, wq_ref, gkv_ref, wk_ref, wv_ref, rope_ref,
                   qa_ref, ka_ref, va_ref, qb_ref, kb_ref, vb_ref,
                   cka_ref, cva_ref, cckv_ref, ckpe_ref):
    i = pl.program_id(0)
    lo = _lane_lo()
    sub = x_ref.shape[0] // INPROJ_SUBTILES
    for t in range(INPROJ_SUBTILES):
        rows = slice(t * sub, (t + 1) * sub)

        def rope(x, tab, quarter):
            return _rope(x, rope_ref[tab, rows, :], rope_ref[tab + 1, rows, :], rope_ref[tab + 2, rows, :], quarter)

        h = (x_ref[rows, :] * (1.0 + sc_ref[...]) + sh_ref[...]).astype(BF16)
        proj = _dot(h, w_in_ref[...])
        qa = proj[:, 0:512]
        ka = proj[:, 512:640]
        va = proj[:, 640:768]
        cq = proj[:, 768:1152]
        ckv = proj[:, 1152:1408]
        kpe = proj[:, 1408:1536]
        ckv_n = _rms_norm(ckv, gkv_ref[...])

        @pl.when(i >= ns_tiles)
        def _():
            cka_ref[rows, :] = ka
            cva_ref[rows, :] = va
            cckv_ref[rows, :] = ckv_n
            ckpe_ref[rows, :] = kpe[:, :B_ROPE]

        qa_ref[rows, :] = (rope(qa, 0, A_HEAD_DIM // 4) * (A_SCALE * LOG2E)).astype(BF16)
        ka_r = rope(ka, 0, A_HEAD_DIM // 4)
        ka_s = pltpu.roll(ka_r, A_HEAD_DIM, 1)
        zero = jnp.zeros_like(ka_r)
        ka_ref[rows, :] = jnp.concatenate(
            [jnp.where(lo, ka_r, zero), jnp.where(lo, zero, ka_s), jnp.where(lo, ka_s, zero),
             jnp.where(lo, zero, ka_r)], axis=1).astype(BF16)
        va_s = pltpu.roll(va, A_HEAD_DIM, 1)
        va_ref[rows, :] = jnp.concatenate([jnp.where(lo, va, va_s), jnp.where(lo, va_s, va)], axis=1).astype(BF16)

        cq_n = _rms_norm(cq, gq_ref[...]).astype(BF16)
        qb = _dot(cq_n, wq_ref[...])
        qb_ref[rows, :] = (rope(qb, 3, B_ROPE // 4) * (B_SCALE * LOG2E)).astype(BF16)
        kpe_r = rope(kpe, 6, B_ROPE // 4)
        lat = jnp.concatenate([ckv_n.astype(BF16), kpe_r.astype(BF16)], axis=1)
        kb_ref[rows, :] = _dot(lat, wk_ref[...]).astype(BF16)
        vb_ref[rows, :] = _dot(lat[:, :B_KV_LORA], wv_ref[...]).astype(BF16)


def _inproj_call(y, modsr, l, w_in, g_cq, w_uq, g_ckv, w_k, w_v, rope_tabs, tm):
    ns_tiles = NS // tm
    pos_tiles = DEC_SEQ // tm

    def tok(w):
        return pl.BlockSpec((tm, w), lambda i: (i, 0))

    def ctx(w):
        return pl.BlockSpec((tm, w), lambda i: (jnp.maximum(i - ns_tiles, 0), 0))

    def full(a):
        return pl.BlockSpec(a.shape, lambda i: (0,) * a.ndim)

    def rope_idx(i):
        return (0, jnp.where(i < ns_tiles, i % pos_tiles, pos_tiles), 0)

    outs = pl.pallas_call(
        functools.partial(_inproj_kernel, ns_tiles),
        grid=(NTOK // tm,),
        in_specs=[
            tok(D), _mod_spec(l, 3, tm, 1), _mod_spec(l, 4, tm, 1),
            full(w_in), full(g_cq), full(w_uq), full(g_ckv), full(w_k), full(w_v),
            pl.BlockSpec((9, tm, LANES), rope_idx),
        ],
        out_specs=[tok(512), tok(512), tok(256), tok(1024), tok(1024), tok(512),
                   ctx(128), ctx(128), ctx(B_KV_LORA), ctx(B_ROPE)],
        out_shape=[
            jax.ShapeDtypeStruct((NTOK, 512), BF16), jax.ShapeDtypeStruct((NTOK, 512), BF16),
            jax.ShapeDtypeStruct((NTOK, 256), BF16), jax.ShapeDtypeStruct((NTOK, 1024), BF16),
            jax.ShapeDtypeStruct((NTOK, 1024), BF16), jax.ShapeDtypeStruct((NTOK, 512), BF16),
            jax.ShapeDtypeStruct((NP, 128), F32), jax.ShapeDtypeStruct((NP, 128), F32),
            jax.ShapeDtypeStruct((NP, B_KV_LORA), F32), jax.ShapeDtypeStruct((NP, B_ROPE), F32),
        ],
        compiler_params=_cparams("arbitrary"),
        name="ab_inproj",
    )(y, modsr, modsr, w_in, g_cq, w_uq, g_ckv, w_k, w_v, rope_tabs)
    return outs


def _ctxkv_kernel(ckv_ref, kpe_ref, wk_ref, wv_ref, k_ref, v_ref):
    ckv = ckv_ref[...].astype(BF16)
    k = _dot(ckv, wk_ref[:B_KV_LORA, :]) + _dot(kpe_ref[...].astype(BF16), wk_ref[B_KV_LORA:B_KV_LORA + B_ROPE, :])
    k_ref[...] = k.astype(BF16)
    v_ref[...] = _dot(ckv, wv_ref[...]).astype(BF16)


def _ctxkv_call(cache_ckv, cache_kpe, li, w_k, w_v):
    return pl.pallas_call(
        _ctxkv_kernel,
        grid=(DEC_BATCH,),
        in_specs=[
            pl.BlockSpec((None, None, PAST, B_KV_LORA), lambda b: (b, li, 0, 0)),
            pl.BlockSpec((None, None, PAST, B_ROPE), lambda b: (b, li, 0, 0)),
            pl.BlockSpec(w_k.shape, lambda b: (0, 0)),
            pl.BlockSpec(w_v.shape, lambda b: (0, 0)),
        ],
        out_specs=[pl.BlockSpec((PAST, 1024), lambda b: (b, 0)), pl.BlockSpec((PAST, 512), lambda b: (b, 0))],
        out_shape=[jax.ShapeDtypeStruct((DEC_BATCH * PAST, 1024), BF16),
                   jax.ShapeDtypeStruct((DEC_BATCH * PAST, 512), BF16)],
        compiler_params=_cparams("arbitrary"),
        name="b_ctx_kv",
    )(cache_ckv, cache_kpe, w_k, w_v)


def _softmax_pv(s, values, sink):
    m = jnp.max(s, axis=-1, keepdims=True)
    if sink is not None:
        m = jnp.maximum(m, sink)
    p = jnp.exp2(s - m)
    den = jnp.sum(p, axis=-1, keepdims=True)
    if sink is not None:
        den = den + jnp.exp2(sink - m)
    p = p.astype(BF16)
    out, c0 = None, 0
    for v in values:
        o = _dot(p[:, c0:c0 + v.shape[0]], v)
        c0 += v.shape[0]
        out = o if out is None else out + o
    return out * (1.0 / den)


def _lane_lo():
    return lax.broadcasted_iota(jnp.int32, (1, LANES), 1) < A_HEAD_DIM


A_LAT_BLOCKS = 4
B_LAT_TQ = 512


def _attn_a_lat_kernel(sink_ref, q_ref, k_ref, v_ref, kc_ref, vc_ref, o_ref):
    nb = DEC_SEQ // BAND
    lo = _lane_lo()
    kc = kc_ref[...]
    vc = vc_ref[...]
    kc_s = pltpu.roll(kc, A_HEAD_DIM, 1)
    vc_s = pltpu.roll(vc, A_HEAD_DIM, 1)
    zero = jnp.zeros_like(kc)
    kc_var = [jnp.where(lo, kc, zero), jnp.where(lo, zero, kc_s), jnp.where(lo, kc_s, zero), jnp.where(lo, zero, kc)]
    kc_var = [k.astype(BF16) for k in kc_var]
    vc_dup = [jnp.where(lo, vc, vc_s).astype(BF16), jnp.where(lo, vc_s, vc).astype(BF16)]

    assert WINDOW == BAND
    qi = lax.broadcasted_iota(jnp.int32, (2 * BAND, BAND), 0) % BAND
    kj = lax.broadcasted_iota(jnp.int32, (2 * BAND, BAND), 1)
    top = lax.broadcasted_iota(jnp.int32, (2 * BAND, 1), 0) < BAND

    for sub in range(A_LAT_BLOCKS):
        n = pl.program_id(1) * A_LAT_BLOCKS + sub
        rows = [pl.multiple_of(jnp.maximum(n - 1, 0) * BAND, BAND), pl.multiple_of(n * BAND, BAND),
                pl.multiple_of(jnp.minimum(n + 1, nb - 1) * BAND, BAND)]
        t_min = jnp.where(n > 0, 0, BAND)
        t_max = jnp.where(n < nb - 1, 3 * BAND - 1, 2 * BAND - 1)
        ok_prev = kj >= jnp.maximum(qi, t_min)
        ok_next = kj + 2 * BAND <= jnp.minimum(qi + 2 * BAND, t_max)
        q0 = sub * BAND
        for hk in range(A_KV_HEADS):
            q2 = jnp.concatenate([q_ref[q0:q0 + BAND, (2 * hk) * LANES:(2 * hk + 1) * LANES],
                                  q_ref[q0:q0 + BAND, (2 * hk + 1) * LANES:(2 * hk + 2) * LANES]], axis=0)
            vs = jnp.concatenate(
                [vc_dup[hk]] + [v_ref[pl.ds(r, BAND), hk * LANES:(hk + 1) * LANES] for r in rows], axis=0)
            res = []
            for e in range(2):
                c0 = (2 * hk + e) * LANES
                ks = jnp.concatenate([kc_var[2 * hk + e]] + [k_ref[pl.ds(r, BAND), c0:c0 + LANES] for r in rows], axis=0)
                s = _dot_nt(q2, ks)
                s = jnp.concatenate([s[:, :PAST], jnp.where(ok_prev, s[:, PAST:PAST + BAND], NEG_INF),
                                     s[:, PAST + BAND:PAST + 2 * BAND],
                                     jnp.where(ok_next, s[:, PAST + 2 * BAND:], NEG_INF)], axis=1)
                sink = jnp.where(top, sink_ref[4 * hk + e], sink_ref[4 * hk + 2 + e]) * LOG2E
                res.append(_softmax_pv(s, [vs], sink))
            o2 = jnp.where(lo, res[0], res[1]).astype(BF16)
            o_ref[q0:q0 + BAND, (2 * hk) * LANES:(2 * hk + 1) * LANES] = o2[:BAND]
            o_ref[q0:q0 + BAND, (2 * hk + 1) * LANES:(2 * hk + 2) * LANES] = o2[BAND:]


def _attn_a_lat_call(sink, qa, ka, va, cache_k, cache_v, li):
    tq = A_LAT_BLOCKS * BAND
    nb = DEC_SEQ // tq
    return pl.pallas_call(
        _attn_a_lat_kernel,
        grid=(DEC_BATCH, nb),
        in_specs=[
            pl.BlockSpec(memory_space=pltpu.SMEM),
            pl.BlockSpec((tq, 512), lambda b, n: (b * nb + n, 0)),
            pl.BlockSpec((DEC_SEQ, 512), lambda b, n: (b, 0)),
            pl.BlockSpec((DEC_SEQ, 256), lambda b, n: (b, 0)),
            pl.BlockSpec((None, None, PAST, 128), lambda b, n: (b, li, 0, 0)),
            pl.BlockSpec((None, None, PAST, 128), lambda b, n: (b, li, 0, 0)),
        ],
        out_specs=pl.BlockSpec((tq, 512), lambda b, n: (b * nb + n, 0)),
        out_shape=jax.ShapeDtypeStruct((NS, 512), BF16),
        compiler_params=_cparams("arbitrary", "arbitrary"),
        name="attn_a_latent",
    )(sink, qa, ka, va, cache_k, cache_v)


def _attn_b_lat_kernel(q_ref, k_ref, v_ref, kc_ref, vc_ref, o_ref):
    lo = _lane_lo()
    for j in range(B_HEADS // 2):
        vs = [vc_ref[:, j * LANES:(j + 1) * LANES], v_ref[:, j * LANES:(j + 1) * LANES]]
        res = []
        for e in range(2):
            c0 = (2 * j + e) * LANES
            q = q_ref[:, c0:c0 + LANES]
            s = jnp.concatenate([_dot_nt(q, kc_ref[:, c0:c0 + LANES]), _dot_nt(q, k_ref[:, c0:c0 + LANES])], axis=1)
            res.append(_softmax_pv(s, vs, None))
        o_ref[:, j * LANES:(j + 1) * LANES] = jnp.where(lo, res[0], res[1]).astype(BF16)


def _attn_b_lat_call(qb, kb, vb, kc, vc):
    tq = B_LAT_TQ
    nq = DEC_SEQ // tq
    return pl.pallas_call(
        _attn_b_lat_kernel,
        grid=(DEC_BATCH, nq),
        in_specs=[
            pl.BlockSpec((tq, 1024), lambda b, n: (b * nq + n, 0)),
            pl.BlockSpec((DEC_SEQ, 1024), lambda b, n: (b, 0)),
            pl.BlockSpec((DEC_SEQ, 512), lambda b, n: (b, 0)),
            pl.BlockSpec((PAST, 1024), lambda b, n: (b, 0)),
            pl.BlockSpec((PAST, 512), lambda b, n: (b, 0)),
        ],
        out_specs=pl.BlockSpec((tq, 512), lambda b, n: (b * nq + n, 0)),
        out_shape=jax.ShapeDtypeStruct((NS, 512), BF16),
        compiler_params=_cparams("arbitrary", "arbitrary"),
        name="attn_b_latent",
    )(qb, kb, vb, kc, vc)


def _attn_ctx_kernel(sink_ref, qa_ref, ka_ref, va_ref, qb_ref, kb_ref, vb_ref, oa_ref, ob_ref):
    lo = _lane_lo()
    top = lax.broadcasted_iota(jnp.int32, (2 * SEQ, 1), 0) < SEQ
    for hk in range(A_KV_HEADS):
        q2 = jnp.concatenate([qa_ref[:, (2 * hk) * LANES:(2 * hk + 1) * LANES],
                              qa_ref[:, (2 * hk + 1) * LANES:(2 * hk + 2) * LANES]], axis=0)
        v = va_ref[:, hk * LANES:(hk + 1) * LANES]
        res = []
        for e in range(2):
            c0 = (2 * hk + e) * LANES
            s = _dot_nt(q2, ka_ref[:, c0:c0 + LANES])
            sink = jnp.where(top, sink_ref[4 * hk + e], sink_ref[4 * hk + 2 + e]) * LOG2E
            res.append(_softmax_pv(s, [v], sink))
        o2 = jnp.where(lo, res[0], res[1]).astype(BF16)
        oa_ref[:, (2 * hk) * LANES:(2 * hk + 1) * LANES] = o2[:SEQ]
        oa_ref[:, (2 * hk + 1) * LANES:(2 * hk + 2) * LANES] = o2[SEQ:]
    for j in range(B_HEADS // 2):
        v = vb_ref[:, j * LANES:(j + 1) * LANES]
        res = []
        for e in range(2):
            c0 = (2 * j + e) * LANES
            s = _dot_nt(qb_ref[:, c0:c0 + LANES], kb_ref[:, c0:c0 + LANES])
            res.append(_softmax_pv(s, [v], None))
        ob_ref[:, j * LANES:(j + 1) * LANES] = jnp.where(lo, res[0], res[1]).astype(BF16)


def _attn_ctx_call(sink, qa, ka, va, qb, kb, vb):
    off = NS // SEQ

    def tok(w):
        return pl.BlockSpec((SEQ, w), lambda b: (off + b, 0))

    def out(w):
        return pl.BlockSpec((SEQ, w), lambda b: (b, 0))

    return pl.pallas_call(
        _attn_ctx_kernel,
        grid=(BATCH,),
        in_specs=[pl.BlockSpec(memory_space=pltpu.SMEM), tok(512), tok(512), tok(256), tok(1024), tok(1024), tok(512)],
        out_specs=[out(512), out(512)],
        out_shape=[jax.ShapeDtypeStruct((NP, 512), BF16), jax.ShapeDtypeStruct((NP, 512), BF16)],
        compiler_params=_cparams("arbitrary"),
        name="attn_context",
    )(sink, qa, ka, va, qb, kb, vb)


def _fourier_in_kernel(x_ref, sh_ref, sc_ref, w_ref, cs_ref, p_ref):
    h = (x_ref[...] * (1.0 + sc_ref[...]) + sh_ref[...]).astype(BF16)
    u = _dot(h, w_ref[...]).astype(BF16)
    for g in range(F_GROUPS):
        pcs = _dot(u[:, g * F_GROUP_DIM:(g + 1) * F_GROUP_DIM], cs_ref[...]).astype(BF16)
        p_ref[:, g * F_GROUP_DIM:(g + 1) * F_GROUP_DIM] = pcs[:, :F_GROUP_DIM]
        p_ref[:, D + g * F_GROUP_DIM:D + (g + 1) * F_GROUP_DIM] = pcs[:, F_GROUP_DIM:]


def _fourier_in_call(y, modsr, l, w_in, cs, tm=512):
    return pl.pallas_call(
        _fourier_in_kernel,
        grid=(NTOK // tm,),
        in_specs=[
            pl.BlockSpec((tm, D), lambda i: (i, 0)), _mod_spec(l, 3, tm, 1), _mod_spec(l, 4, tm, 1),
            pl.BlockSpec(w_in.shape, lambda i: (0, 0)), pl.BlockSpec(cs.shape, lambda i: (0, 0)),
        ],
        out_specs=pl.BlockSpec((tm, 2 * D), lambda i: (i, 0)),
        out_shape=jax.ShapeDtypeStruct((NTOK, 2 * D), BF16),
        compiler_params=_cparams("arbitrary"),
        name="fourier_in",
    )(y, modsr, modsr, w_in, cs)


def _fourier_pos_kernel(norm, c_ref, s_ref, p_ref, f_ref):
    f = _dot(c_ref[...], p_ref[:, :D]) + _dot(s_ref[...], p_ref[:, D:])
    f_ref[...] = (f * norm).astype(BF16)


def _fourier_pos_call(p, cmat, smat, length, nbatch, row0):
    tk = min(length, 256)
    nk = length // tk
    off = row0 // length
    norm = float((length * F_GROUP_DIM) ** -0.5)
    return pl.pallas_call(
        functools.partial(_fourier_pos_kernel, norm),
        grid=(nbatch, nk),
        in_specs=[
            pl.BlockSpec((tk, length), lambda b, k: (k, 0)),
            pl.BlockSpec((tk, length), lambda b, k: (k, 0)),
            pl.BlockSpec((length, 2 * D), lambda b, k: (off + b, 0)),
        ],
        out_specs=pl.BlockSpec((tk, D), lambda b, k: (b * nk + k, 0)),
        out_shape=jax.ShapeDtypeStruct((nbatch * length, D), BF16),
        compiler_params=_cparams("arbitrary", "arbitrary"),
        name="fourier_pos",
    )(cmat, smat, p)


CT_N1 = 4
CT_N2 = DEC_SEQ // CT_N1
CT_ROWS = 64
CT_COLS = 256


def _ct_tables():
    n2 = np.arange(CT_N2)
    tw = np.zeros((2, CT_N1 - 1, CT_N2, LANES), np.float32)
    for k1 in range(1, CT_N1):
        ang = 2.0 * np.pi * ((n2 * k1) % DEC_SEQ).astype(np.float64) / DEC_SEQ
        tw[0, k1 - 1] = np.cos(ang)[:, None]
        tw[1, k1 - 1] = np.sin(ang)[:, None]
    c2, s2 = _dft_mats(CT_N2)
    return tw, np.concatenate([c2, -s2], axis=1)


def _fourier_pos_ct_kernel(norm, p_ref, tw_ref, cs_ref, f_ref, b_scr, nat_scr):
    n2 = CT_N2
    for q in range(D // CT_COLS):
        re = slice(q * CT_COLS, (q + 1) * CT_COLS)
        im = slice(D + q * CT_COLS, D + (q + 1) * CT_COLS)
        for r0 in range(0, n2, CT_ROWS):
            a = [p_ref[j * n2 + r0:j * n2 + r0 + CT_ROWS, re].astype(F32) for j in range(CT_N1)]
            b = [p_ref[j * n2 + r0:j * n2 + r0 + CT_ROWS, im].astype(F32) for j in range(CT_N1)]
            s02a, d02a, s13a, d13a = a[0] + a[2], a[0] - a[2], a[1] + a[3], a[1] - a[3]
            s02b, d02b, s13b, d13b = b[0] + b[2], b[0] - b[2], b[1] + b[3], b[1] - b[3]
            br = [s02a + s13a, d02a + d13b, s02a - s13a, d02a - d13b]
            bi = [s02b + s13b, d02b - d13a, s02b - s13b, d02b + d13a]
            for k1 in range(CT_N1):
                r, i = br[k1], bi[k1]
                if k1 > 0:
                    tc = jnp.concatenate([tw_ref[0, k1 - 1, r0:r0 + CT_ROWS, :]] * (CT_COLS // LANES), axis=1)
                    ts = jnp.concatenate([tw_ref[1, k1 - 1, r0:r0 + CT_ROWS, :]] * (CT_COLS // LANES), axis=1)
                    r, i = r * tc + i * ts, i * tc - r * ts
                b_scr[k1, r0:r0 + CT_ROWS, re] = r.astype(BF16)
                b_scr[k1, n2 + r0:n2 + r0 + CT_ROWS, re] = i.astype(BF16)
        slabs = CT_COLS // LANES
        nat = [nat_scr.at[(q % 2) * slabs + j] for j in range(slabs)]
        for k1 in range(CT_N1):
            f = _dot(cs_ref[...], b_scr[k1, :, re]) * norm
            for j in range(slabs):
                nat[j][pl.ds(k1, n2, stride=CT_N1), :] = f[:, j * LANES:(j + 1) * LANES]
        f_ref[:, re] = jnp.concatenate([nat[j][...] for j in range(slabs)], axis=1).astype(BF16)


def _fourier_pos_ct_call(p, tw, cs):
    norm = float((DEC_SEQ * F_GROUP_DIM) ** -0.5)
    return pl.pallas_call(
        functools.partial(_fourier_pos_ct_kernel, norm),
        grid=(DEC_BATCH,),
        in_specs=[
            pl.BlockSpec((DEC_SEQ, 2 * D), lambda b: (b, 0)),
            pl.BlockSpec(tw.shape, lambda b: (0, 0, 0, 0)),
            pl.BlockSpec(cs.shape, lambda b: (0, 0)),
        ],
        out_specs=pl.BlockSpec((DEC_SEQ, D), lambda b: (b, 0)),
        out_shape=jax.ShapeDtypeStruct((NS, D), BF16),
        scratch_shapes=[pltpu.VMEM((CT_N1, 2 * CT_N2, D), BF16), pltpu.VMEM((2 * CT_COLS // LANES, DEC_SEQ, LANES), F32)],
        compiler_params=_cparams("arbitrary"),
        name="fourier_pos_ct",
    )(p, tw, cs)


def kernel(x_prompt, x_sample, cache_a_k, cache_a_v, cache_b_ckv, cache_b_kpe, c, c_ctx, w_ada, b_ada, ln_g, ln_b,
           ffn_w_gate, ffn_w_up, ffn_w_down, ab_w_in, ab_w_out, a_sink, b_g_cq, b_w_uq, b_g_ckv, b_w_ukv,
           f_w_in, f_w_out):
    tm_in = INPROJ_TM
    y = (x_sample.reshape(NS, D), x_prompt.reshape(NP, D))
    cond = jnp.concatenate([c, c_ctx[None, :], jnp.zeros((MOD_ROWS - DEC_BATCH - 1, D), F32)], axis=0)
    lng = ln_g.reshape(DEPTH * 3, 1, D)
    lnb = ln_b.reshape(DEPTH * 3, 1, D)
    wg = ffn_w_gate.astype(BF16)
    wu = ffn_w_up.astype(BF16)
    wd = ffn_w_down.astype(BF16)
    w_out = ab_w_out.astype(BF16)
    w_in, g_cq, w_uq, g_ckv, w_k, w_v = _prep_ab_weights(ab_w_in, b_g_cq, b_w_uq, b_g_ckv, b_w_ukv)
    fw_in = f_w_in.astype(BF16)
    fw_out = f_w_out.astype(BF16)
    cache_k = cache_a_k.reshape(DEC_BATCH, N_AB, PAST, A_KV_HEADS * A_HEAD_DIM)
    cache_v = cache_a_v.reshape(DEC_BATCH, N_AB, PAST, A_KV_HEADS * A_HEAD_DIM)
    rope_tabs = jnp.asarray(_rope_tables(tm_in))
    cc, sc = _dft_mats(F_GROUP_DIM)
    cs_chan = jnp.asarray(np.concatenate([cc, sc], axis=1), dtype=BF16)
    cp, sp = _dft_mats(SEQ)
    dft_ctx = (jnp.asarray(cp, dtype=BF16), jnp.asarray(-sp, dtype=BF16))
    ct_tw, ct_cs = _ct_tables()
    ct_tw = jnp.asarray(ct_tw)
    ct_cs = jnp.asarray(ct_cs, dtype=BF16)

    mods = _mods_call(cond, w_ada, b_ada)
    modsr = mods.reshape(DEPTH * MOD_ROWS * N_MODS, 1, D)

    new_k, new_v, new_ckv, new_kpe = [], [], [], []
    for l in range(DEPTH):
        li = l // 2
        y = _ffn_call(y, modsr, lng, lnb, wg, wu, wd, l, 0)
        if l % 2 == 0:
            (qa, ka, va, qb, kb, vb, cka, cva, cckv, ckpe) = _inproj_call(
                y, modsr, l, w_in[li], g_cq[li], w_uq[li], g_ckv[li], w_k[li], w_v[li], rope_tabs, tm_in)
            kc, vc = _ctxkv_call(cache_b_ckv, cache_b_kpe, li, w_k[li], w_v[li])
            oa_s = _attn_a_lat_call(a_sink[li], qa, ka, va, cache_k, cache_v, li)
            ob_s = _attn_b_lat_call(qb, kb, vb, kc, vc)
            oa_p, ob_p = _attn_ctx_call(a_sink[li], qa, ka, va, qb, kb, vb)
            parts = [
                (oa_s, oa_p, w_out, pl.BlockSpec((None, 512, D), lambda i, li=li: (li, 0, 0))),
                (ob_s, ob_p, w_out, pl.BlockSpec((None, 512, D), lambda i, li=li: (li, 1, 0))),
            ]
            new_k.append(cka.reshape(BATCH, SEQ, A_KV_HEADS, A_HEAD_DIM))
            new_v.append(cva.reshape(BATCH, SEQ, A_KV_HEADS, A_HEAD_DIM))
            new_ckv.append(cckv.reshape(BATCH, SEQ, B_KV_LORA))
            new_kpe.append(ckpe.reshape(BATCH, SEQ, B_ROPE))
        else:
            p = _fourier_in_call(y, modsr, l, fw_in[li], cs_chan)
            f_s = _fourier_pos_ct_call(p, ct_tw, ct_cs)
            f_p = _fourier_pos_call(p, dft_ctx[0], dft_ctx[1], SEQ, BATCH, NS)
            parts = [(f_s, f_p, fw_out, pl.BlockSpec((None, D, D), lambda i, li=li: (li, 0, 0)))]
        y = _ffn_call(y, modsr, lng, lnb, wg, wu, wd, l, 1, split_out=(l == DEPTH - 1), mixer_parts=parts)

    y_sample = y[0].reshape(DEC_BATCH, DEC_SEQ, D)
    y_prompt = y[1].reshape(BATCH, SEQ, D)
    return (y_prompt, y_sample, jnp.stack(new_k, axis=1), jnp.stack(new_v, axis=1),
            jnp.stack(new_ckv, axis=1), jnp.stack(new_kpe, axis=1))
```

```python
import functools

import numpy as np
import jax
import jax.numpy as jnp
from jax import lax
from jax.experimental import pallas as pl
from jax.experimental.pallas import tpu as pltpu

F32 = jnp.float32
BF16 = jnp.bfloat16

D = 1024
BATCH = 16
SEQ = 256
DEPTH = 4
DEC_BATCH = 8
DEC_SEQ = 2048
PAST = 256
GRID_W = 64
N_AB = (DEPTH + 1) // 2
A_HEADS = 8
A_KV_HEADS = 2
A_HEAD_DIM = 64
WINDOW = 128
BAND = 128
B_HEADS = 8
B_NOPE = 64
B_ROPE = 32
B_V = 64
B_Q_LORA = 384
B_KV_LORA = 256
AB_IN = 1440
AB_IN_PAD = 1536
A_SCALE = A_HEAD_DIM ** -0.5
B_SCALE = (B_NOPE + B_ROPE) ** -0.5
F_GROUPS = 4
F_GROUP_DIM = D // F_GROUPS
D_FF = 2816
FFN_HALF = 0.5
ALPHA = (2 * DEPTH) ** 0.25
LN_EPS = 1e-5
RMS_EPS = 1e-6
N_MODS = 9
ROPE_BASE = 10000.0
NEG_INF = -1e30
LOG2E = 1.4426950408889634

NS = DEC_BATCH * DEC_SEQ
NP = BATCH * SEQ
NTOK = NS + NP
MOD_ROWS = 16
CTX_ROW = DEC_BATCH
LANES = 128
VMEM_LIMIT = 56 * 1024 * 1024


def _cparams(*sem):
    return pltpu.CompilerParams(dimension_semantics=sem, vmem_limit_bytes=VMEM_LIMIT)


def _silu(x):
    return x * (1.0 / (1.0 + jnp.exp(-x)))


def _layer_norm(z, g, b):
    mu = jnp.mean(z, axis=-1, keepdims=True)
    zc = z - mu
    var = jnp.mean(zc * zc, axis=-1, keepdims=True)
    return zc * lax.rsqrt(var + LN_EPS) * g + b


def _rms_norm(x, g):
    return x * lax.rsqrt(jnp.mean(x * x, axis=-1, keepdims=True) + RMS_EPS) * g


def _dot(a, b):
    return jnp.dot(a, b, preferred_element_type=F32)


def _dot_nt(a, b):
    return lax.dot_general(a, b, (((1,), (1,)), ((), ())), preferred_element_type=F32)


def _rope_tables(tm):
    pos = np.arange(DEC_SEQ)
    row = (pos // GRID_W).astype(np.float64)
    col = (pos % GRID_W).astype(np.float64)

    def layout(lane_d, quarter):
        cos = np.ones((DEC_SEQ, LANES))
        sin_m = np.zeros((DEC_SEQ, LANES))
        sin_p = np.zeros((DEC_SEQ, LANES))
        for lane in range(LANES):
            d = lane_d[lane]
            if d < 0:
                continue
            q, j = d // quarter, d % quarter
            inv = ROPE_BASE ** (-float(j) / quarter)
            ang = (row if q < 2 else col) * inv
            cos[:, lane] = np.cos(ang)
            if q % 2 == 0:
                sin_m[:, lane] = -np.sin(ang)
            else:
                sin_p[:, lane] = np.sin(ang)
        return [cos, sin_m, sin_p]

    lane_a = [lane % A_HEAD_DIM for lane in range(LANES)]
    lane_b = [lane - B_NOPE if B_NOPE <= lane < B_NOPE + B_ROPE else -1 for lane in range(LANES)]
    lane_k = [lane if lane < B_ROPE else -1 for lane in range(LANES)]
    tabs = layout(lane_a, A_HEAD_DIM // 4) + layout(lane_b, B_ROPE // 4) + layout(lane_k, B_ROPE // 4)
    out = np.zeros((9, DEC_SEQ + tm, LANES), np.float32)
    for t, tab in enumerate(tabs):
        out[t, :DEC_SEQ] = tab
        out[t, DEC_SEQ:] = 1.0 if t % 3 == 0 else 0.0
    return out


def _dft_mats(n):
    k = np.arange(n)
    kn = (k[:, None] * k[None, :]) % n
    ang = 2.0 * np.pi * kn.astype(np.float64) / n
    return np.cos(ang), -np.sin(ang)


def _mods_kernel(c_ref, w_ref, b_ref, o_ref):
    x = _silu(c_ref[...]).astype(BF16)
    o_ref[...] = _dot(x, w_ref[...].astype(BF16)) + b_ref[...]


def _mods_call(cond, w_ada, b_ada):
    tn = 1024
    nw = N_MODS * D
    return pl.pallas_call(
        _mods_kernel,
        grid=(DEPTH, nw // tn),
        in_specs=[
            pl.BlockSpec((MOD_ROWS, D), lambda l, j: (0, 0)),
            pl.BlockSpec((None, D, tn), lambda l, j: (l, 0, j)),
            pl.BlockSpec((None, 1, tn), lambda l, j: (l, 0, j)),
        ],
        out_specs=pl.BlockSpec((None, MOD_ROWS, tn), lambda l, j: (l, 0, j)),
        out_shape=jax.ShapeDtypeStruct((DEPTH, MOD_ROWS, nw), F32),
        compiler_params=_cparams("arbitrary", "arbitrary"),
        name="ada_mods",
    )(cond, w_ada, b_ada.reshape(DEPTH, 1, nw))


def _mod_spec(l, m, tm, ngrid):
    def idx(*g):
        r = jnp.minimum((g[0] * tm) // DEC_SEQ, CTX_ROW)
        return (l * MOD_ROWS * N_MODS + r * N_MODS + m, 0, 0)
    del ngrid
    return pl.BlockSpec((None, 1, D), idx)


def _ln_spec(l, which):
    return pl.BlockSpec((None, 1, D), lambda *g: (l * 3 + which, 0, 0))


FFN_TM = 512
FFN_TF = 256
FFN_SUBTILES = 2


def _ffn_kernel(ns_tiles, split_in, split_out, nparts, *refs):
    refs = list(refs)
    x_refs = [refs.pop(0) for _ in range(2 if split_in else 1)]
    if nparts:
        mix_gt_ref = refs.pop(0)
        part_refs = [refs.pop(0) for _ in range(2 * nparts)]
        part_w_refs = [refs.pop(0) for _ in range(nparts)]
        mix_g_ref, mix_b_ref = refs.pop(0), refs.pop(0)
    sh_ref, sc_ref, gt_ref, wg_ref, wu_ref, wd_ref, g_ref, b_ref = [refs.pop(0) for _ in range(8)]
    o_refs = [refs.pop(0) for _ in range(2 if split_out else 1)]
    h_scr, a_scr = refs.pop(0), refs.pop(0)
    latent = pl.program_id(0) < ns_tiles

    y_scr = refs.pop(0) if nparts else None
    sub = h_scr.shape[0] // FFN_SUBTILES
    halves = [slice(t * sub, (t + 1) * sub) for t in range(FFN_SUBTILES)]

    def load_in(rows):
        if split_in:
            return jnp.where(latent, x_refs[0][rows, :], x_refs[1][rows, :])
        return x_refs[0][rows, :]

    load_x = load_in
    if nparts:
        def mix_dot(rows):
            mixed = None
            for k in range(nparts):
                part = jnp.where(latent, part_refs[2 * k][rows, :], part_refs[2 * k + 1][rows, :])
                p = _dot(part, part_w_refs[k][...])
                mixed = p if mixed is None else mixed + p
            return mixed

        for rows, mixed in [(rows, mix_dot(rows)) for rows in halves]:
            y_scr[rows, :] = _layer_norm(ALPHA * load_in(rows) + mix_gt_ref[...] * mixed, mix_g_ref[...], mix_b_ref[...])

        def load_x(rows):
            return y_scr[rows, :]

    for rows in halves:
        h_scr[rows, :] = (load_x(rows) * (1.0 + sc_ref[...]) + sh_ref[...]).astype(BF16)
    for c in range(D_FF // FFN_TF):
        cols = slice(c * FFN_TF, (c + 1) * FFN_TF)
        for rows in halves:
            h = h_scr[rows, :]
            a_scr[rows, cols] = (_silu(_dot(h, wg_ref[:, cols])) * _dot(h, wu_ref[:, cols])).astype(BF16)
    for rows, down in [(rows, _dot(a_scr[rows, :], wd_ref[...])) for rows in halves]:
        z = ALPHA * load_x(rows) + gt_ref[...] * (FFN_HALF * down)
        out = _layer_norm(z, g_ref[...], b_ref[...])
        if split_out:
            @pl.when(latent)
            def _():
                o_refs[0][rows, :] = out

            @pl.when(jnp.logical_not(latent))
            def _():
                o_refs[1][rows, :] = out
        else:
            o_refs[0][rows, :] = out


def _stream_specs(tm, w, ns_tiles):
    return [pl.BlockSpec((tm, w), lambda i: (jnp.minimum(i, ns_tiles - 1), 0)),
            pl.BlockSpec((tm, w), lambda i: (jnp.maximum(i - ns_tiles, 0), 0))]


def _ffn_call(ys, modsr, lng, lnb, wg, wu, wd, l, k, split_out=False, mixer_parts=()):
    tm = FFN_TM
    ns_tiles = NS // tm
    split_in = isinstance(ys, tuple)
    m0 = 0 if k == 0 else 6
    which = 0 if k == 0 else 2
    joint = pl.BlockSpec((tm, D), lambda i: (i, 0))
    resident = pl.Buffered(1)
    in_specs = _stream_specs(tm, D, ns_tiles) if split_in else [joint]
    args = list(ys) if split_in else [ys]
    scratch = [pltpu.VMEM((tm, D), BF16), pltpu.VMEM((tm, D_FF), BF16)]
    if mixer_parts:
        in_specs.append(_mod_spec(l, 5, tm, 1))
        args.append(modsr)
        for xs, xp, _, _ in mixer_parts:
            in_specs += _stream_specs(tm, xs.shape[1], ns_tiles)
            args += [xs, xp]
        for _, _, warr, wspec in mixer_parts:
            in_specs.append(wspec)
            args.append(warr)
        in_specs += [_ln_spec(l, 1), _ln_spec(l, 1)]
        args += [lng, lnb]
        scratch.append(pltpu.VMEM((tm, D), F32))
    in_specs += [
        _mod_spec(l, m0, tm, 1), _mod_spec(l, m0 + 1, tm, 1), _mod_spec(l, m0 + 2, tm, 1),
        pl.BlockSpec((None, None, D, D_FF), lambda i: (l, k, 0, 0), pipeline_mode=resident),
        pl.BlockSpec((None, None, D, D_FF), lambda i: (l, k, 0, 0), pipeline_mode=resident),
        pl.BlockSpec((None, None, D_FF, D), lambda i: (l, k, 0, 0), pipeline_mode=resident),
        _ln_spec(l, which), _ln_spec(l, which),
    ]
    args += [modsr, modsr, modsr, wg, wu, wd, lng, lnb]
    if split_out:
        out_specs = _stream_specs(tm, D, ns_tiles)
        out_shape = [jax.ShapeDtypeStruct((NS, D), F32), jax.ShapeDtypeStruct((NP, D), F32)]
    else:
        out_specs = joint
        out_shape = jax.ShapeDtypeStruct((NTOK, D), F32)
    return pl.pallas_call(
        functools.partial(_ffn_kernel, ns_tiles, split_in, split_out, len(mixer_parts)),
        grid=(NTOK // tm,),
        in_specs=in_specs,
        out_specs=out_specs,
        out_shape=out_shape,
        scratch_shapes=scratch,
        compiler_params=_cparams("arbitrary"),
        name="half_ffn",
    )(*args)


INPROJ_TM = 1024
INPROJ_SUBTILES = 2


def _prep_ab_weights(ab_w_in, b_g_cq, b_w_uq, b_g_ckv, b_w_ukv):
    w_in = jnp.pad(ab_w_in, ((0, 0), (0, 0), (0, AB_IN_PAD - AB_IN))).astype(BF16)
    w_uq = jnp.pad(b_w_uq.reshape(N_AB, B_Q_LORA, B_HEADS, B_NOPE + B_ROPE),
                   ((0, 0), (0, 0), (0, 0), (0, LANES - B_NOPE - B_ROPE))).reshape(N_AB, B_Q_LORA, 1024).astype(BF16)
    ukv = b_w_ukv.reshape(N_AB, B_KV_LORA, B_HEADS, B_NOPE + B_V)
    w_uk = jnp.pad(ukv[..., :B_NOPE], ((0, 0), (0, 0), (0, 0), (0, LANES - B_NOPE))).reshape(N_AB, B_KV_LORA, 1024)
    place = np.zeros((LANES, B_HEADS, LANES), np.float32)
    for d in range(B_ROPE):
        place[d, :, B_NOPE + d] = 1.0
    w_k = jnp.concatenate(
        [w_uk, jnp.broadcast_to(jnp.asarray(place.reshape(LANES, 1024)), (N_AB, LANES, 1024))], axis=1).astype(BF16)
    w_v = ukv[..., B_NOPE:].reshape(N_AB, B_KV_LORA, 512).astype(BF16)
    g_cq = b_g_cq.reshape(N_AB, 1, B_Q_LORA)
    g_ckv = b_g_ckv.reshape(N_AB, 1, B_KV_LORA)
    return w_in, g_cq, w_uq, g_ckv, w_k, w_v


def _rope(x, cos, sin_m, sin_p, quarter):
    n = x.shape[1] // LANES
    up = pltpu.roll(x, x.shape[1] - quarter, 1)
    dn = pltpu.roll(x, quarter, 1)
    if n > 1:
        cos = jnp.concatenate([cos] * n, axis=1)
        sin_m = jnp.concatenate([sin_m] * n, axis=1)
        sin_p = jnp.concatenate([sin_p] * n, axis=1)
    return x * cos + up * sin_m + dn * sin_p


def _inproj_kernel(ns_tiles, x_ref, sh_ref, sc_ref, w_in_ref, gq_ref, wq_ref, gkv_ref, wk_ref, wv_ref, rope_ref,
                   qa_ref, ka_ref, va_ref, qb_ref, kb_ref, vb_ref,
                   cka_ref, cva_ref, cckv_ref, ckpe_ref):
    i = pl.program_id(0)
    lo = _lane_lo()
    sub = x_ref.shape[0] // INPROJ_SUBTILES
    for t in range(INPROJ_SUBTILES):
        rows = slice(t * sub, (t + 1) * sub)

        def rope(x, tab, quarter):
            return _rope(x, rope_ref[tab, rows, :], rope_ref[tab + 1, rows, :], rope_ref[tab + 2, rows, :], quarter)

        h = (x_ref[rows, :] * (1.0 + sc_ref[...]) + sh_ref[...]).astype(BF16)
        proj = _dot(h, w_in_ref[...])
        qa = proj[:, 0:512]
        ka = proj[:, 512:640]
        va = proj[:, 640:768]
        cq = proj[:, 768:1152]
        ckv = proj[:, 1152:1408]
        kpe = proj[:, 1408:1536]
        ckv_n = _rms_norm(ckv, gkv_ref[...])

        @pl.when(i >= ns_tiles)
        def _():
            cka_ref[rows, :] = ka
            cva_ref[rows, :] = va
            cckv_ref[rows, :] = ckv_n
            ckpe_ref[rows, :] = kpe[:, :B_ROPE]

        qa_ref[rows, :] = (rope(qa, 0, A_HEAD_DIM // 4) * (A_SCALE * LOG2E)).astype(BF16)
        ka_r = rope(ka, 0, A_HEAD_DIM // 4)
        ka_s = pltpu.roll(ka_r, A_HEAD_DIM, 1)
        zero = jnp.zeros_like(ka_r)
        ka_ref[rows, :] = jnp.concatenate(
            [jnp.where(lo, ka_r, zero), jnp.where(lo, zero, ka_s), jnp.where(lo, ka_s, zero),
             jnp.where(lo, zero, ka_r)], axis=1).astype(BF16)
        va_s = pltpu.roll(va, A_HEAD_DIM, 1)
        va_ref[rows, :] = jnp.concatenate([jnp.where(lo, va, va_s), jnp.where(lo, va_s, va)], axis=1).astype(BF16)

        cq_n = _rms_norm(cq, gq_ref[...]).astype(BF16)
        qb = _dot(cq_n, wq_ref[...])
        qb_ref[rows, :] = (rope(qb, 3, B_ROPE // 4) * (B_SCALE * LOG2E)).astype(BF16)
        kpe_r = rope(kpe, 6, B_ROPE // 4)
        lat = jnp.concatenate([ckv_n.astype(BF16), kpe_r.astype(BF16)], axis=1)
        kb_ref[rows, :] = _dot(lat, wk_ref[...]).astype(BF16)
        vb_ref[rows, :] = _dot(lat[:, :B_KV_LORA], wv_ref[...]).astype(BF16)


def _inproj_call(y, modsr, l, w_in, g_cq, w_uq, g_ckv, w_k, w_v, rope_tabs, tm):
    ns_tiles = NS // tm
    pos_tiles = DEC_SEQ // tm

    def tok(w):
        return pl.BlockSpec((tm, w), lambda i: (i, 0))

    def ctx(w):
        return pl.BlockSpec((tm, w), lambda i: (jnp.maximum(i - ns_tiles, 0), 0))

    def full(a):
        return pl.BlockSpec(a.shape, lambda i: (0,) * a.ndim)

    def rope_idx(i):
        return (0, jnp.where(i < ns_tiles, i % pos_tiles, pos_tiles), 0)

    outs = pl.pallas_call(
        functools.partial(_inproj_kernel, ns_tiles),
        grid=(NTOK // tm,),
        in_specs=[
            tok(D), _mod_spec(l, 3, tm, 1), _mod_spec(l, 4, tm, 1),
            full(w_in), full(g_cq), full(w_uq), full(g_ckv), full(w_k), full(w_v),
            pl.BlockSpec((9, tm, LANES), rope_idx),
        ],
        out_specs=[tok(512), tok(512), tok(256), tok(1024), tok(1024), tok(512),
                   ctx(128), ctx(128), ctx(B_KV_LORA), ctx(B_ROPE)],
        out_shape=[
            jax.ShapeDtypeStruct((NTOK, 512), BF16), jax.ShapeDtypeStruct((NTOK, 512), BF16),
            jax.ShapeDtypeStruct((NTOK, 256), BF16), jax.ShapeDtypeStruct((NTOK, 1024), BF16),
            jax.ShapeDtypeStruct((NTOK, 1024), BF16), jax.ShapeDtypeStruct((NTOK, 512), BF16),
            jax.ShapeDtypeStruct((NP, 128), F32), jax.ShapeDtypeStruct((NP, 128), F32),
            jax.ShapeDtypeStruct((NP, B_KV_LORA), F32), jax.ShapeDtypeStruct((NP, B_ROPE), F32),
        ],
        compiler_params=_cparams("arbitrary"),
        name="ab_inproj",
    )(y, modsr, modsr, w_in, g_cq, w_uq, g_ckv, w_k, w_v, rope_tabs)
    return outs


def _ctxkv_kernel(ckv_ref, kpe_ref, wk_ref, wv_ref, k_ref, v_ref):
    ckv = ckv_ref[...].astype(BF16)
    k = _dot(ckv, wk_ref[:B_KV_LORA, :]) + _dot(kpe_ref[...].astype(BF16), wk_ref[B_KV_LORA:B_KV_LORA + B_ROPE, :])
    k_ref[...] = k.astype(BF16)
    v_ref[...] = _dot(ckv, wv_ref[...]).astype(BF16)


def _ctxkv_call(cache_ckv, cache_kpe, li, w_k, w_v):
    return pl.pallas_call(
        _ctxkv_kernel,
        grid=(DEC_BATCH,),
        in_specs=[
            pl.BlockSpec((None, None, PAST, B_KV_LORA), lambda b: (b, li, 0, 0)),
            pl.BlockSpec((None, None, PAST, B_ROPE), lambda b: (b, li, 0, 0)),
            pl.BlockSpec(w_k.shape, lambda b: (0, 0)),
            pl.BlockSpec(w_v.shape, lambda b: (0, 0)),
        ],
        out_specs=[pl.BlockSpec((PAST, 1024), lambda b: (b, 0)), pl.BlockSpec((PAST, 512), lambda b: (b, 0))],
        out_shape=[jax.ShapeDtypeStruct((DEC_BATCH * PAST, 1024), BF16),
                   jax.ShapeDtypeStruct((DEC_BATCH * PAST, 512), BF16)],
        compiler_params=_cparams("arbitrary"),
        name="b_ctx_kv",
    )(cache_ckv, cache_kpe, w_k, w_v)


def _softmax_pv(s, values, sink):
    m = jnp.max(s, axis=-1, keepdims=True)
    if sink is not None:
        m = jnp.maximum(m, sink)
    p = jnp.exp2(s - m)
    den = jnp.sum(p, axis=-1, keepdims=True)
    if sink is not None:
        den = den + jnp.exp2(sink - m)
    p = p.astype(BF16)
    out, c0 = None, 0
    for v in values:
        o = _dot(p[:, c0:c0 + v.shape[0]], v)
        c0 += v.shape[0]
        out = o if out is None else out + o
    return out * (1.0 / den)


def _lane_lo():
    return lax.broadcasted_iota(jnp.int32, (1, LANES), 1) < A_HEAD_DIM


A_LAT_BLOCKS = 4
B_LAT_TQ = 512


def _attn_a_lat_kernel(sink_ref, q_ref, k_ref, v_ref, kc_ref, vc_ref, o_ref):
    nb = DEC_SEQ // BAND
    lo = _lane_lo()
    kc = kc_ref[...]
    vc = vc_ref[...]
    kc_s = pltpu.roll(kc, A_HEAD_DIM, 1)
    vc_s = pltpu.roll(vc, A_HEAD_DIM, 1)
    zero = jnp.zeros_like(kc)
    kc_var = [jnp.where(lo, kc, zero), jnp.where(lo, zero, kc_s), jnp.where(lo, kc_s, zero), jnp.where(lo, zero, kc)]
    kc_var = [k.astype(BF16) for k in kc_var]
    vc_dup = [jnp.where(lo, vc, vc_s).astype(BF16), jnp.where(lo, vc_s, vc).astype(BF16)]

    assert WINDOW == BAND
    qi = lax.broadcasted_iota(jnp.int32, (2 * BAND, BAND), 0) % BAND
    kj = lax.broadcasted_iota(jnp.int32, (2 * BAND, BAND), 1)
    top = lax.broadcasted_iota(jnp.int32, (2 * BAND, 1), 0) < BAND

    def block(sub):
        n = pl.program_id(1) * A_LAT_BLOCKS + sub
        rows = [pl.multiple_of(jnp.maximum(n - 1, 0) * BAND, BAND), pl.multiple_of(n * BAND, BAND),
                pl.multiple_of(jnp.minimum(n + 1, nb - 1) * BAND, BAND)]
        t_min = jnp.where(n > 0, 0, BAND)
        t_max = jnp.where(n < nb - 1, 3 * BAND - 1, 2 * BAND - 1)
        return rows, kj >= jnp.maximum(qi, t_min), kj + 2 * BAND <= jnp.minimum(qi + 2 * BAND, t_max)

    blocks = [block(sub) for sub in range(A_LAT_BLOCKS)]
    chains = [(sub, hk, e) for sub in range(A_LAT_BLOCKS) for hk in range(A_KV_HEADS) for e in range(2)]

    def scores(i):
        sub, hk, e = chains[i]
        rows, ok_prev, ok_next = blocks[sub]
        q0 = sub * BAND
        c0 = (2 * hk + e) * LANES
        q2 = jnp.concatenate([q_ref[q0:q0 + BAND, (2 * hk) * LANES:(2 * hk + 1) * LANES],
                              q_ref[q0:q0 + BAND, (2 * hk + 1) * LANES:(2 * hk + 2) * LANES]], axis=0)
        ks = jnp.concatenate([kc_var[2 * hk + e]] + [k_ref[pl.ds(r, BAND), c0:c0 + LANES] for r in rows], axis=0)
        s = _dot_nt(q2, ks)
        return jnp.concatenate([s[:, :PAST], jnp.where(ok_prev, s[:, PAST:PAST + BAND], NEG_INF),
                                s[:, PAST + BAND:PAST + 2 * BAND],
                                jnp.where(ok_next, s[:, PAST + 2 * BAND:], NEG_INF)], axis=1)

    even = {}

    def attend(i, s):
        sub, hk, e = chains[i]
        rows = blocks[sub][0]
        q0 = sub * BAND
        vs = jnp.concatenate([vc_dup[hk]] + [v_ref[pl.ds(r, BAND), hk * LANES:(hk + 1) * LANES] for r in rows], axis=0)
        sink = jnp.where(top, sink_ref[4 * hk + e], sink_ref[4 * hk + 2 + e]) * LOG2E
        o = _softmax_pv(s, [vs], sink)
        if e == 0:
            even[(sub, hk)] = o
        else:
            o2 = jnp.where(lo, even.pop((sub, hk)), o).astype(BF16)
            o_ref[q0:q0 + BAND, (2 * hk) * LANES:(2 * hk + 1) * LANES] = o2[:BAND]
            o_ref[q0:q0 + BAND, (2 * hk + 1) * LANES:(2 * hk + 2) * LANES] = o2[BAND:]

    _run_ahead(len(chains), scores, attend, ahead=2)


def _attn_a_lat_call(sink, qa, ka, va, cache_k, cache_v, li):
    tq = A_LAT_BLOCKS * BAND
    nb = DEC_SEQ // tq
    return pl.pallas_call(
        _attn_a_lat_kernel,
        grid=(DEC_BATCH, nb),
        in_specs=[
            pl.BlockSpec(memory_space=pltpu.SMEM),
            pl.BlockSpec((tq, 512), lambda b, n: (b * nb + n, 0)),
            pl.BlockSpec((DEC_SEQ, 512), lambda b, n: (b, 0)),
            pl.BlockSpec((DEC_SEQ, 256), lambda b, n: (b, 0)),
            pl.BlockSpec((None, None, PAST, 128), lambda b, n: (b, li, 0, 0)),
            pl.BlockSpec((None, None, PAST, 128), lambda b, n: (b, li, 0, 0)),
        ],
        out_specs=pl.BlockSpec((tq, 512), lambda b, n: (b * nb + n, 0)),
        out_shape=jax.ShapeDtypeStruct((NS, 512), BF16),
        compiler_params=_cparams("arbitrary", "arbitrary"),
        name="attn_a_latent",
    )(sink, qa, ka, va, cache_k, cache_v)


def _run_ahead(n, first_stage, second_stage, ahead=1):
    pending = [first_stage(i) for i in range(min(ahead, n))]
    for i in range(n):
        if i + ahead < n:
            pending.append(first_stage(i + ahead))
        second_stage(i, pending.pop(0))


def _attn_b_lat_kernel(q_ref, k_ref, v_ref, kc_ref, vc_ref, o_ref):
    lo = _lane_lo()

    def scores(h):
        cols = slice(h * LANES, (h + 1) * LANES)
        q = q_ref[:, cols]
        return jnp.concatenate([_dot_nt(q, kc_ref[:, cols]), _dot_nt(q, k_ref[:, cols])], axis=1)

    even = {}

    def attend(h, s):
        pair = slice((h // 2) * LANES, (h // 2 + 1) * LANES)
        o = _softmax_pv(s, [vc_ref[:, pair], v_ref[:, pair]], None)
        if h % 2 == 0:
            even[h // 2] = o
        else:
            o_ref[:, pair] = jnp.where(lo, even.pop(h // 2), o).astype(BF16)

    _run_ahead(B_HEADS, scores, attend)


def _attn_b_lat_call(qb, kb, vb, kc, vc):
    tq = B_LAT_TQ
    nq = DEC_SEQ // tq
    return pl.pallas_call(
        _attn_b_lat_kernel,
        grid=(DEC_BATCH, nq),
        in_specs=[
            pl.BlockSpec((tq, 1024), lambda b, n: (b * nq + n, 0)),
            pl.BlockSpec((DEC_SEQ, 1024), lambda b, n: (b, 0)),
            pl.BlockSpec((DEC_SEQ, 512), lambda b, n: (b, 0)),
            pl.BlockSpec((PAST, 1024), lambda b, n: (b, 0)),
            pl.BlockSpec((PAST, 512), lambda b, n: (b, 0)),
        ],
        out_specs=pl.BlockSpec((tq, 512), lambda b, n: (b * nq + n, 0)),
        out_shape=jax.ShapeDtypeStruct((NS, 512), BF16),
        compiler_params=_cparams("arbitrary", "arbitrary"),
        name="attn_b_latent",
    )(qb, kb, vb, kc, vc)


def _attn_ctx_kernel(sink_ref, qa_ref, ka_ref, va_ref, qb_ref, kb_ref, vb_ref, oa_ref, ob_ref):
    lo = _lane_lo()
    top = lax.broadcasted_iota(jnp.int32, (2 * SEQ, 1), 0) < SEQ
    for hk in range(A_KV_HEADS):
        q2 = jnp.concatenate([qa_ref[:, (2 * hk) * LANES:(2 * hk + 1) * LANES],
                              qa_ref[:, (2 * hk + 1) * LANES:(2 * hk + 2) * LANES]], axis=0)
        v = va_ref[:, hk * LANES:(hk + 1) * LANES]
        res = []
        for e in range(2):
            c0 = (2 * hk + e) * LANES
            s = _dot_nt(q2, ka_ref[:, c0:c0 + LANES])
            sink = jnp.where(top, sink_ref[4 * hk + e], sink_ref[4 * hk + 2 + e]) * LOG2E
            res.append(_softmax_pv(s, [v], sink))
        o2 = jnp.where(lo, res[0], res[1]).astype(BF16)
        oa_ref[:, (2 * hk) * LANES:(2 * hk + 1) * LANES] = o2[:SEQ]
        oa_ref[:, (2 * hk + 1) * LANES:(2 * hk + 2) * LANES] = o2[SEQ:]
    for j in range(B_HEADS // 2):
        v = vb_ref[:, j * LANES:(j + 1) * LANES]
        res = []
        for e in range(2):
            c0 = (2 * j + e) * LANES
            s = _dot_nt(qb_ref[:, c0:c0 + LANES], kb_ref[:, c0:c0 + LANES])
            res.append(_softmax_pv(s, [v], None))
        ob_ref[:, j * LANES:(j + 1) * LANES] = jnp.where(lo, res[0], res[1]).astype(BF16)


def _attn_ctx_call(sink, qa, ka, va, qb, kb, vb):
    off = NS // SEQ

    def tok(w):
        return pl.BlockSpec((SEQ, w), lambda b: (off + b, 0))

    def out(w):
        return pl.BlockSpec((SEQ, w), lambda b: (b, 0))

    return pl.pallas_call(
        _attn_ctx_kernel,
        grid=(BATCH,),
        in_specs=[pl.BlockSpec(memory_space=pltpu.SMEM), tok(512), tok(512), tok(256), tok(1024), tok(1024), tok(512)],
        out_specs=[out(512), out(512)],
        out_shape=[jax.ShapeDtypeStruct((NP, 512), BF16), jax.ShapeDtypeStruct((NP, 512), BF16)],
        compiler_params=_cparams("arbitrary"),
        name="attn_context",
    )(sink, qa, ka, va, qb, kb, vb)


def _fourier_in_kernel(x_ref, sh_ref, sc_ref, w_ref, cs_ref, p_ref):
    h = (x_ref[...] * (1.0 + sc_ref[...]) + sh_ref[...]).astype(BF16)
    u = _dot(h, w_ref[...]).astype(BF16)
    for g in range(F_GROUPS):
        pcs = _dot(u[:, g * F_GROUP_DIM:(g + 1) * F_GROUP_DIM], cs_ref[...]).astype(BF16)
        p_ref[:, g * F_GROUP_DIM:(g + 1) * F_GROUP_DIM] = pcs[:, :F_GROUP_DIM]
        p_ref[:, D + g * F_GROUP_DIM:D + (g + 1) * F_GROUP_DIM] = pcs[:, F_GROUP_DIM:]


def _fourier_in_call(y, modsr, l, w_in, cs, tm=512):
    return pl.pallas_call(
        _fourier_in_kernel,
        grid=(NTOK // tm,),
        in_specs=[
            pl.BlockSpec((tm, D), lambda i: (i, 0)), _mod_spec(l, 3, tm, 1), _mod_spec(l, 4, tm, 1),
            pl.BlockSpec(w_in.shape, lambda i: (0, 0)), pl.BlockSpec(cs.shape, lambda i: (0, 0)),
        ],
        out_specs=pl.BlockSpec((tm, 2 * D), lambda i: (i, 0)),
        out_shape=jax.ShapeDtypeStruct((NTOK, 2 * D), BF16),
        compiler_params=_cparams("arbitrary"),
        name="fourier_in",
    )(y, modsr, modsr, w_in, cs)


def _fourier_pos_kernel(norm, c_ref, s_ref, p_ref, f_ref):
    f = _dot(c_ref[...], p_ref[:, :D]) + _dot(s_ref[...], p_ref[:, D:])
    f_ref[...] = (f * norm).astype(BF16)


def _fourier_pos_call(p, cmat, smat, length, nbatch, row0):
    tk = min(length, 256)
    nk = length // tk
    off = row0 // length
    norm = float((length * F_GROUP_DIM) ** -0.5)
    return pl.pallas_call(
        functools.partial(_fourier_pos_kernel, norm),
        grid=(nbatch, nk),
        in_specs=[
            pl.BlockSpec((tk, length), lambda b, k: (k, 0)),
            pl.BlockSpec((tk, length), lambda b, k: (k, 0)),
            pl.BlockSpec((length, 2 * D), lambda b, k: (off + b, 0)),
        ],
        out_specs=pl.BlockSpec((tk, D), lambda b, k: (b * nk + k, 0)),
        out_shape=jax.ShapeDtypeStruct((nbatch * length, D), BF16),
        compiler_params=_cparams("arbitrary", "arbitrary"),
        name="fourier_pos",
    )(cmat, smat, p)


CT_N1 = 4
CT_N2 = DEC_SEQ // CT_N1
CT_ROWS = 64
CT_COLS = 256


def _ct_tables():
    n2 = np.arange(CT_N2)
    tw = np.zeros((2, CT_N1 - 1, CT_N2, LANES), np.float32)
    for k1 in range(1, CT_N1):
        ang = 2.0 * np.pi * ((n2 * k1) % DEC_SEQ).astype(np.float64) / DEC_SEQ
        tw[0, k1 - 1] = np.cos(ang)[:, None]
        tw[1, k1 - 1] = np.sin(ang)[:, None]
    c2, s2 = _dft_mats(CT_N2)
    return tw, np.concatenate([c2, -s2], axis=1)


def _fourier_pos_ct_kernel(norm, p_ref, tw_ref, cs_ref, f_ref, b_scr, nat_scr):
    n2 = CT_N2
    for q in range(D // CT_COLS):
        re = slice(q * CT_COLS, (q + 1) * CT_COLS)
        im = slice(D + q * CT_COLS, D + (q + 1) * CT_COLS)
        for r0 in range(0, n2, CT_ROWS):
            a = [p_ref[j * n2 + r0:j * n2 + r0 + CT_ROWS, re].astype(F32) for j in range(CT_N1)]
            b = [p_ref[j * n2 + r0:j * n2 + r0 + CT_ROWS, im].astype(F32) for j in range(CT_N1)]
            s02a, d02a, s13a, d13a = a[0] + a[2], a[0] - a[2], a[1] + a[3], a[1] - a[3]
            s02b, d02b, s13b, d13b = b[0] + b[2], b[0] - b[2], b[1] + b[3], b[1] - b[3]
            br = [s02a + s13a, d02a + d13b, s02a - s13a, d02a - d13b]
            bi = [s02b + s13b, d02b - d13a, s02b - s13b, d02b + d13a]
            for k1 in range(CT_N1):
                r, i = br[k1], bi[k1]
                if k1 > 0:
                    tc = jnp.concatenate([tw_ref[0, k1 - 1, r0:r0 + CT_ROWS, :]] * (CT_COLS // LANES), axis=1)
                    ts = jnp.concatenate([tw_ref[1, k1 - 1, r0:r0 + CT_ROWS, :]] * (CT_COLS // LANES), axis=1)
                    r, i = r * tc + i * ts, i * tc - r * ts
                b_scr[k1, r0:r0 + CT_ROWS, re] = r.astype(BF16)
                b_scr[k1, n2 + r0:n2 + r0 + CT_ROWS, re] = i.astype(BF16)
        slabs = CT_COLS // LANES
        nat = [nat_scr.at[(q % 2) * slabs + j] for j in range(slabs)]
        for k1 in range(CT_N1):
            f = _dot(cs_ref[...], b_scr[k1, :, re]) * norm
            for j in range(slabs):
                nat[j][pl.ds(k1, n2, stride=CT_N1), :] = f[:, j * LANES:(j + 1) * LANES]
        f_ref[:, re] = jnp.concatenate([nat[j][...] for j in range(slabs)], axis=1).astype(BF16)


def _fourier_pos_ct_call(p, tw, cs):
    norm = float((DEC_SEQ * F_GROUP_DIM) ** -0.5)
    return pl.pallas_call(
        functools.partial(_fourier_pos_ct_kernel, norm),
        grid=(DEC_BATCH,),
        in_specs=[
            pl.BlockSpec((DEC_SEQ, 2 * D), lambda b: (b, 0)),
            pl.BlockSpec(tw.shape, lambda b: (0, 0, 0, 0)),
            pl.BlockSpec(cs.shape, lambda b: (0, 0)),
        ],
        out_specs=pl.BlockSpec((DEC_SEQ, D), lambda b: (b, 0)),
        out_shape=jax.ShapeDtypeStruct((NS, D), BF16),
        scratch_shapes=[pltpu.VMEM((CT_N1, 2 * CT_N2, D), BF16), pltpu.VMEM((2 * CT_COLS // LANES, DEC_SEQ, LANES), F32)],
        compiler_params=_cparams("arbitrary"),
        name="fourier_pos_ct",
    )(p, tw, cs)


def kernel(x_prompt, x_sample, cache_a_k, cache_a_v, cache_b_ckv, cache_b_kpe, c, c_ctx, w_ada, b_ada, ln_g, ln_b,
           ffn_w_gate, ffn_w_up, ffn_w_down, ab_w_in, ab_w_out, a_sink, b_g_cq, b_w_uq, b_g_ckv, b_w_ukv,
           f_w_in, f_w_out):
    tm_in = INPROJ_TM
    y = (x_sample.reshape(NS, D), x_prompt.reshape(NP, D))
    cond = jnp.concatenate([c, c_ctx[None, :], jnp.zeros((MOD_ROWS - DEC_BATCH - 1, D), F32)], axis=0)
    lng = ln_g.reshape(DEPTH * 3, 1, D)
    lnb = ln_b.reshape(DEPTH * 3, 1, D)
    wg = ffn_w_gate.astype(BF16)
    wu = ffn_w_up.astype(BF16)
    wd = ffn_w_down.astype(BF16)
    w_out = ab_w_out.astype(BF16)
    w_in, g_cq, w_uq, g_ckv, w_k, w_v = _prep_ab_weights(ab_w_in, b_g_cq, b_w_uq, b_g_ckv, b_w_ukv)
    fw_in = f_w_in.astype(BF16)
    fw_out = f_w_out.astype(BF16)
    cache_k = cache_a_k.reshape(DEC_BATCH, N_AB, PAST, A_KV_HEADS * A_HEAD_DIM)
    cache_v = cache_a_v.reshape(DEC_BATCH, N_AB, PAST, A_KV_HEADS * A_HEAD_DIM)
    rope_tabs = jnp.asarray(_rope_tables(tm_in))
    cc, sc = _dft_mats(F_GROUP_DIM)
    cs_chan = jnp.asarray(np.concatenate([cc, sc], axis=1), dtype=BF16)
    cp, sp = _dft_mats(SEQ)
    dft_ctx = (jnp.asarray(cp, dtype=BF16), jnp.asarray(-sp, dtype=BF16))
    ct_tw, ct_cs = _ct_tables()
    ct_tw = jnp.asarray(ct_tw)
    ct_cs = jnp.asarray(ct_cs, dtype=BF16)

    mods = _mods_call(cond, w_ada, b_ada)
    modsr = mods.reshape(DEPTH * MOD_ROWS * N_MODS, 1, D)

    new_k, new_v, new_ckv, new_kpe = [], [], [], []
    for l in range(DEPTH):
        li = l // 2
        y = _ffn_call(y, modsr, lng, lnb, wg, wu, wd, l, 0)
        if l % 2 == 0:
            (qa, ka, va, qb, kb, vb, cka, cva, cckv, ckpe) = _inproj_call(
                y, modsr, l, w_in[li], g_cq[li], w_uq[li], g_ckv[li], w_k[li], w_v[li], rope_tabs, tm_in)
            kc, vc = _ctxkv_call(cache_b_ckv, cache_b_kpe, li, w_k[li], w_v[li])
            oa_s = _attn_a_lat_call(a_sink[li], qa, ka, va, cache_k, cache_v, li)
            ob_s = _attn_b_lat_call(qb, kb, vb, kc, vc)
            oa_p, ob_p = _attn_ctx_call(a_sink[li], qa, ka, va, qb, kb, vb)
            parts = [
                (oa_s, oa_p, w_out, pl.BlockSpec((None, 512, D), lambda i, li=li: (li, 0, 0))),
                (ob_s, ob_p, w_out, pl.BlockSpec((None, 512, D), lambda i, li=li: (li, 1, 0))),
            ]
            new_k.append(cka.reshape(BATCH, SEQ, A_KV_HEADS, A_HEAD_DIM))
            new_v.append(cva.reshape(BATCH, SEQ, A_KV_HEADS, A_HEAD_DIM))
            new_ckv.append(cckv.reshape(BATCH, SEQ, B_KV_LORA))
            new_kpe.append(ckpe.reshape(BATCH, SEQ, B_ROPE))
        else:
            p = _fourier_in_call(y, modsr, l, fw_in[li], cs_chan)
            f_s = _fourier_pos_ct_call(p, ct_tw, ct_cs)
            f_p = _fourier_pos_call(p, dft_ctx[0], dft_ctx[1], SEQ, BATCH, NS)
            parts = [(f_s, f_p, fw_out, pl.BlockSpec((None, D, D), lambda i, li=li: (li, 0, 0)))]
        y = _ffn_call(y, modsr, lng, lnb, wg, wu, wd, l, 1, split_out=(l == DEPTH - 1), mixer_parts=parts)

    y_sample = y[0].reshape(DEC_BATCH, DEC_SEQ, D)
    y_prompt = y[1].reshape(BATCH, SEQ, D)
    return (y_prompt, y_sample, jnp.stack(new_k, axis=1), jnp.stack(new_v, axis=1),
            jnp.stack(new_ckv, axis=1), jnp.stack(new_kpe, axis=1))
```
